```python
import jax
import jax.numpy as jnp
from jax import lax
import numpy as np

D_MODEL = 2048
BATCH = 4
SEQ = 2048
DEPTH = 2
DEC_BATCH = 8
DEC_SEQ = 1
PAST_LEN = 16384
PAGE_SIZE = 128

DH = 128
ROT_DIM = DH // 4
ROPE_THETA = 500000.0
H_A = 8
KVH_A = 2
G_A = H_A // KVH_A
H_I = 16
D_IDX = 64
ROT_IDX = D_IDX // 4
TOPK_MAX = 256
H_B = 8
KVH_B = 2
G_B = H_B // KVH_B
L_CMP = 32
S_CMP = 16
CMP_HID = DH
L_SLC = 64
N_SEL_MAX = 16
WINDOW = 512
D_FF = 5632
N_EXP = 8
TOP_E = 2
D_FF_E = 7168
N_DENSE = (DEPTH + 1) // 2
N_MOE = DEPTH // 2
Q_BLK = 128
SLC_QBLK = 32
EPS = 1e-6
NEG = -1e30
FORCED = 1e6

_IN_SPLITS = (
    ('q_a', H_A * DH), ('k_a', KVH_A * DH), ('v_a', KVH_A * DH),
    ('q_i', H_I * D_IDX), ('k_i', D_IDX), ('w_i', H_I),
    ('q_b', H_B * DH),
    ('k_cmp', KVH_B * DH), ('v_cmp', KVH_B * DH),
    ('k_slc', KVH_B * DH), ('v_slc', KVH_B * DH),
    ('k_win', KVH_B * DH), ('v_win', KVH_B * DH),
    ('g_b', 3 * H_B), ('g_mix', 2 * D_MODEL),
)
N_IN = sum(w for _, w in _IN_SPLITS)

kernel_name = 'dsa_nsa_hybrid_decode_step'


def _rms(x, g):
    xf = x.astype(jnp.float32)
    y = xf * lax.rsqrt(jnp.mean(xf * xf, axis=-1, keepdims=True) + EPS)
    return (y * g.astype(jnp.float32)).astype(x.dtype)


def _rope(x, pos, rot):
    half = rot // 2
    inv = jnp.power(jnp.float32(ROPE_THETA), -jnp.arange(half, dtype=jnp.float32) / half)
    ang = pos.astype(jnp.float32)[:, None] * inv[None, :]
    cos = jnp.cos(ang)[:, None, :]
    sin = jnp.sin(ang)[:, None, :]
    xr = x[..., :rot].astype(jnp.float32)
    x1, x2 = xr[..., :half], xr[..., half:]
    r = jnp.concatenate([x1 * cos - x2 * sin, x2 * cos + x1 * sin], axis=-1).astype(x.dtype)
    return jnp.concatenate([r, x[..., rot:]], axis=-1)


def _masked_probs(s, mask):
    s = jnp.where(mask, s.astype(jnp.float32), NEG)
    m = jnp.max(s, axis=-1, keepdims=True)
    p = jnp.where(mask, jnp.exp(s - m), 0.0)
    return p / jnp.maximum(jnp.sum(p, axis=-1, keepdims=True), 1e-30)


def _attn_gathered(q, k, v, mask):
    s = jnp.einsum('btkgd,btknd->btkgn', q, k, preferred_element_type=jnp.float32) * DH ** -0.5
    p = _masked_probs(s, mask)
    return jnp.einsum('btkgn,btknd->btkgd', p.astype(v.dtype), v)


def _attn_shared(q, k, v, mask):
    s = jnp.einsum('btkgd,bnkd->btkgn', q, k, preferred_element_type=jnp.float32) * DH ** -0.5
    p = _masked_probs(s, mask)
    return jnp.einsum('btkgn,bnkd->btkgd', p.astype(v.dtype), v)


def _project(h, w_in, pos):
    B, T, _ = h.shape
    z = h @ w_in
    parts = {}
    off = 0
    for name, width in _IN_SPLITS:
        parts[name] = z[..., off:off + width]
        off += width

    def heads(name, n, d, rot):
        return _rope(parts[name].reshape(B, T, n, d), pos, rot)

    return {
        'q_a': heads('q_a', H_A, DH, ROT_DIM).reshape(B, T, KVH_A, G_A, DH),
        'k_a': heads('k_a', KVH_A, DH, ROT_DIM),
        'v_a': parts['v_a'].reshape(B, T, KVH_A, DH),
        'q_i': heads('q_i', H_I, D_IDX, ROT_IDX),
        'k_i': heads('k_i', 1, D_IDX, ROT_IDX)[:, :, 0],
        'w_i': parts['w_i'],
        'q_b': heads('q_b', H_B, DH, ROT_DIM).reshape(B, T, KVH_B, G_B, DH),
        'k_cmp': heads('k_cmp', KVH_B, DH, ROT_DIM),
        'v_cmp': parts['v_cmp'].reshape(B, T, KVH_B, DH),
        'k_slc': heads('k_slc', KVH_B, DH, ROT_DIM),
        'v_slc': parts['v_slc'].reshape(B, T, KVH_B, DH),
        'k_win': heads('k_win', KVH_B, DH, ROT_DIM),
        'v_win': parts['v_win'].reshape(B, T, KVH_B, DH),
        'g_b': jax.nn.sigmoid(parts['g_b'].reshape(B, T, KVH_B, G_B, 3)),
        'g_mix': jax.nn.sigmoid(parts['g_mix'].reshape(B, T, 2, D_MODEL)),
    }


def _paged_rows(pool, page_table):
    g = pool[page_table]
    return g.reshape(page_table.shape[0], PAST_LEN, *pool.shape[2:])


def _index_scores(qi, wi, ki):
    dots = jnp.einsum('bthd,bnd->bthn', qi, ki, preferred_element_type=jnp.float32) * D_IDX ** -0.5
    return jnp.einsum('bthn,bth->btn', jax.nn.relu(dots), wi.astype(jnp.float32) * H_I ** -0.5)


def _dsa_select(scores, qpos, kpos, k_sel):
    adm = kpos[None, None, :] <= qpos[None, :, None]
    val, idx = lax.top_k(jnp.where(adm, scores, NEG), k_sel)
    return idx, val > NEG * 0.5


def _dsa_prompt(q, k, v, qi, wi, ki):
    B, S = k.shape[:2]
    k_sel = min(TOPK_MAX, S // 4)
    kpos = jnp.arange(S)
    take = jax.vmap(lambda rows, ix: rows[ix])

    def blk(i):
        s0 = i * Q_BLK
        qb = lax.dynamic_slice_in_dim(q, s0, Q_BLK, 1)
        qib = lax.dynamic_slice_in_dim(qi, s0, Q_BLK, 1)
        wib = lax.dynamic_slice_in_dim(wi, s0, Q_BLK, 1)
        idx, valid = _dsa_select(_index_scores(qib, wib, ki), s0 + jnp.arange(Q_BLK), kpos, k_sel)
        kg = jnp.swapaxes(take(k, idx), 2, 3)
        vg = jnp.swapaxes(take(v, idx), 2, 3)
        return _attn_gathered(qb, kg, vg, valid[:, :, None, None, :])

    o = lax.map(blk, jnp.arange(S // Q_BLK))
    return jnp.moveaxis(o, 0, 1).reshape(B, S, KVH_A, G_A, DH)


def _gather_rows(pool, page_table, new, idx):
    past = idx < PAST_LEN
    ip = jnp.where(past, idx, 0)
    phys = jax.vmap(lambda pt, ix: pt[ix])(page_table, ip // PAGE_SIZE)
    from_pool = pool[phys, ip % PAGE_SIZE]
    from_new = jax.vmap(lambda rows, ix: rows[ix])(new, jnp.where(past, 0, idx - PAST_LEN))
    return jnp.where(past[..., None, None], from_pool, from_new)


def _dsa_sample(q, k_new, v_new, qi, wi, ki_new, pool_k, pool_v, pool_ki, page_table):
    T = k_new.shape[1]
    L = PAST_LEN + T
    k_sel = min(TOPK_MAX, L // 4)
    ki = jnp.concatenate([_paged_rows(pool_ki, page_table), ki_new], axis=1)
    idx, valid = _dsa_select(_index_scores(qi, wi, ki), PAST_LEN + jnp.arange(T), jnp.arange(L), k_sel)
    kg = jnp.swapaxes(_gather_rows(pool_k, page_table, k_new, idx), 2, 3)
    vg = jnp.swapaxes(_gather_rows(pool_v, page_table, v_new, idx), 2, 3)
    return _attn_gathered(q, kg, vg, valid[:, :, None, None, :])


def _compress(x, pe, w1, w2):
    B, N = x.shape[:2]
    r = L_CMP // S_CMP
    nch = N // S_CMP
    nc = nch - r + 1
    ch = x[:, :nch * S_CMP].reshape(B, nch, S_CMP, KVH_B, DH)
    blocks = jnp.concatenate([ch[:, i:i + nc] for i in range(r)], axis=2)
    blocks = blocks + pe[None, None, :, None, :].astype(x.dtype)
    flat = jnp.swapaxes(blocks, 2, 3).reshape(B, nc, KVH_B, L_CMP * DH)
    return jax.nn.gelu(flat @ w1) @ w2


def _overlap(nc, nslc):
    cs = jnp.arange(nc)[:, None] * S_CMP
    js = jnp.arange(nslc)[None, :] * L_SLC
    return ((cs < js + L_SLC) & (cs + L_CMP > js)).astype(jnp.float32)


def _nsa_cmp(qg, kc, vc, qpos):
    nc = kc.shape[1]
    vis = (jnp.arange(nc) * S_CMP + L_CMP - 1)[None, :] <= qpos[:, None]
    s = jnp.einsum('btkgd,bnkd->btkgn', qg, kc, preferred_element_type=jnp.float32) * DH ** -0.5
    pr = _masked_probs(s, vis[None, :, None, None, :])
    return jnp.einsum('btkgn,bnkd->btkgd', pr.astype(vc.dtype), vc), pr


def _nsa_select(imp, qpos, nslc, nsel):
    j = jnp.arange(nslc)[None, None, None, :]
    jt = (qpos // L_SLC)[None, :, None, None]
    forced = (j == 0) | (j == jt) | (j == jt - 1)
    sc = jnp.where(j > jt, NEG, jnp.where(forced, FORCED, imp))
    val, idx = lax.top_k(sc, nsel)
    return idx, val > NEG * 0.5


def _nsa_cmp_select(qg, k_raw, v_raw, cw, qpos, L):
    kc = _compress(k_raw, cw[0], cw[1], cw[2])
    vc = _compress(v_raw, cw[3], cw[4], cw[5])
    o_cmp, pc = _nsa_cmp(qg, kc, vc, qpos)
    nslc = -(-L // L_SLC)
    imp = jnp.einsum('btkgn,nj->btkj', pc, _overlap(kc.shape[1], nslc))
    idx, valid = _nsa_select(imp, qpos, nslc, min(N_SEL_MAX, nslc))
    return o_cmp, idx, valid


def _slc_attend(qg, kg, vg, idx, valid, qpos):
    B, T, KV, NS = idx.shape
    tpos = idx[..., None] * L_SLC + jnp.arange(L_SLC)
    mask = (tpos <= qpos[None, :, None, None, None]) & valid[..., None]
    n = NS * L_SLC
    return _attn_gathered(qg, kg.reshape(B, T, KV, n, DH), vg.reshape(B, T, KV, n, DH),
                          mask.reshape(B, T, KV, 1, n))


def _nsa_slc_prompt(qg, k, v, idx, valid):
    B, S = k.shape[:2]
    kb = k.reshape(B, S // L_SLC, L_SLC, KVH_B, DH)
    vb = v.reshape(B, S // L_SLC, L_SLC, KVH_B, DH)
    bi = jnp.arange(B)[:, None, None, None]
    hi = jnp.arange(KVH_B)[None, None, :, None]

    def blk(i):
        s0 = i * SLC_QBLK
        qb = lax.dynamic_slice_in_dim(qg, s0, SLC_QBLK, 1)
        ib = lax.dynamic_slice_in_dim(idx, s0, SLC_QBLK, 1)
        vl = lax.dynamic_slice_in_dim(valid, s0, SLC_QBLK, 1)
        kg = kb[bi, ib, :, hi]
        vg = vb[bi, ib, :, hi]
        return _slc_attend(qb, kg, vg, ib, vl, s0 + jnp.arange(SLC_QBLK))

    o = lax.map(blk, jnp.arange(S // SLC_QBLK))
    return jnp.moveaxis(o, 0, 1).reshape(B, S, KVH_B, G_B, DH)


def _gather_blocks_sample(pool, page_table, new, idx):
    DB, T = new.shape[:2]
    bpp = PAGE_SIZE // L_SLC
    npb = PAST_LEN // L_SLC
    nt = -(-T // L_SLC)
    tail = jnp.pad(new, ((0, 0), (0, nt * L_SLC - T), (0, 0), (0, 0))).reshape(DB, nt, L_SLC, KVH_B, DH)
    past = idx < npb
    jp = jnp.where(past, idx, 0)
    phys = jax.vmap(lambda pt, ix: pt[ix])(page_table, jp // bpp)
    view = pool.reshape(pool.shape[0], bpp, L_SLC, KVH_B, DH)
    hi = jnp.arange(KVH_B)[None, None, :, None]
    bi = jnp.arange(DB)[:, None, None, None]
    from_pool = view[phys, jp % bpp, :, hi]
    from_tail = tail[bi, jnp.where(past, 0, idx - npb), :, hi]
    return jnp.where(past[..., None, None], from_pool, from_tail)


def _win_prompt(qg, k, v):
    B, S = k.shape[:2]
    kp = jnp.pad(k, ((0, 0), (WINDOW, 0), (0, 0), (0, 0)))
    vp = jnp.pad(v, ((0, 0), (WINDOW, 0), (0, 0), (0, 0)))

    def blk(i):
        s0 = i * Q_BLK
        qb = lax.dynamic_slice_in_dim(qg, s0, Q_BLK, 1)
        kb = lax.dynamic_slice_in_dim(kp, s0, WINDOW + Q_BLK, 1)
        vb = lax.dynamic_slice_in_dim(vp, s0, WINDOW + Q_BLK, 1)
        kpos = s0 - WINDOW + jnp.arange(WINDOW + Q_BLK)
        qpos = s0 + jnp.arange(Q_BLK)
        m = (kpos[None, :] <= qpos[:, None]) & (kpos[None, :] > qpos[:, None] - WINDOW) & (kpos[None, :] >= 0)
        return _attn_shared(qb, kb, vb, m[None, :, None, None, :])

    o = lax.map(blk, jnp.arange(S // Q_BLK))
    return jnp.moveaxis(o, 0, 1).reshape(B, S, KVH_B, G_B, DH)


def _win_sample(qg, k_new, v_new, buf_k, buf_v):
    T = k_new.shape[1]
    W = buf_k.shape[1]
    kc = jnp.concatenate([buf_k, k_new], axis=1)
    vc = jnp.concatenate([buf_v, v_new], axis=1)
    kpos = PAST_LEN - W + jnp.arange(W + T)
    qpos = PAST_LEN + jnp.arange(T)
    m = (kpos[None, :] <= qpos[:, None]) & (kpos[None, :] > qpos[:, None] - WINDOW)
    o = _attn_shared(qg, kc, vc, m[None, :, None, None, :])
    return o, kc[:, T:], vc[:, T:]


def _merge(x, p, o_a, o_cmp, o_slc, o_win, w_up_a, w_up_b, w_o):
    B, T, _ = x.shape
    g = p['g_b']
    o_b = g[..., 0:1] * o_cmp + g[..., 1:2] * o_slc + g[..., 2:3] * o_win
    gm = p['g_mix']
    u = gm[:, :, 0] * (o_a.reshape(B, T, H_A * DH) @ w_up_a) + gm[:, :, 1] * (o_b.reshape(B, T, H_B * DH) @ w_up_b)
    return x + u @ w_o


def _swiglu(h, wg, wu, wd):
    return (jax.nn.silu(h @ wg) * (h @ wu)) @ wd


def _moe(h, w_r, wg, wu, wd):
    logits = (h @ w_r).astype(jnp.float32)
    val, idx = lax.top_k(logits, TOP_E)
    gates = jax.nn.softmax(val, axis=-1)
    comb = jnp.sum(jax.nn.one_hot(idx, N_EXP, dtype=jnp.float32) * gates[..., None], axis=-2).astype(h.dtype)
    out = jnp.zeros_like(h)
    for e in range(N_EXP):
        out = out + comb[..., e:e + 1] * _swiglu(h, wg[e], wu[e], wd[e])
    return out


def _channel(x, l, norm_ffn, dg, du, dd, mr, mg, mu, md):
    h = _rms(x, norm_ffn[l])
    i = l // 2
    if l % 2 == 0:
        f = _swiglu(h, dg[i], du[i], dd[i])
    else:
        f = _moe(h, mr[i], mg[i], mu[i], md[i])
    return x + f


def setup_inputs(seed: int = 0) -> dict:
    key = jax.random.key(seed)
    ks = iter(jax.random.split(key, 48))

    def nrm(shape, scale):
        return jax.random.normal(next(ks), shape, jnp.float32) * scale

    n_pages = PAST_LEN // PAGE_SIZE
    n_used = DEC_BATCH * n_pages
    n_pool = n_used + max(1, n_used // 4)
    w_buf = min(WINDOW, PAST_LEN)
    page_table = jax.random.permutation(next(ks), n_pool)[:n_used].reshape(DEC_BATCH, n_pages).astype(jnp.int32)
    kv_a = (DEPTH, n_pool, PAGE_SIZE, KVH_A, DH)
    kv_b = (DEPTH, n_pool, PAGE_SIZE, KVH_B, DH)
    return {
        'x_prompt': nrm((BATCH, SEQ, D_MODEL), 1.0),
        'x_sample': nrm((DEC_BATCH, DEC_SEQ, D_MODEL), 1.0),
        'cache_dsa_k': nrm(kv_a, 1.0),
        'cache_dsa_v': nrm(kv_a, 1.0),
        'cache_idx_k': nrm((DEPTH, n_pool, PAGE_SIZE, D_IDX), 1.0),
        'cache_cmp_k': nrm(kv_b, 1.0),
        'cache_cmp_v': nrm(kv_b, 1.0),
        'cache_slc_k': nrm(kv_b, 1.0),
        'cache_slc_v': nrm(kv_b, 1.0),
        'state_win_k': nrm((DEPTH, DEC_BATCH, w_buf, KVH_B, DH), 1.0),
        'state_win_v': nrm((DEPTH, DEC_BATCH, w_buf, KVH_B, DH), 1.0),
        'page_table': page_table,
        'norm_mix': 1.0 + nrm((DEPTH, D_MODEL), 0.01),
        'w_in': nrm((DEPTH, D_MODEL, N_IN), D_MODEL ** -0.5),
        'cmp_pe_k': nrm((DEPTH, L_CMP, DH), 0.1),
        'cmp_w1_k': nrm((DEPTH, L_CMP * DH, CMP_HID), (L_CMP * DH) ** -0.5),
        'cmp_w2_k': nrm((DEPTH, CMP_HID, DH), CMP_HID ** -0.5),
        'cmp_pe_v': nrm((DEPTH, L_CMP, DH), 0.1),
        'cmp_w1_v': nrm((DEPTH, L_CMP * DH, CMP_HID), (L_CMP * DH) ** -0.5),
        'cmp_w2_v': nrm((DEPTH, CMP_HID, DH), CMP_HID ** -0.5),
        'w_up_a': nrm((DEPTH, H_A * DH, D_MODEL), (H_A * DH) ** -0.5),
        'w_up_b': nrm((DEPTH, H_B * DH, D_MODEL), (H_B * DH) ** -0.5),
        'w_o': nrm((DEPTH, D_MODEL, D_MODEL), D_MODEL ** -0.5),
        'norm_ffn': 1.0 + nrm((DEPTH, D_MODEL), 0.01),
        'dense_w_gate': nrm((N_DENSE, D_MODEL, D_FF), D_MODEL ** -0.5),
        'dense_w_up': nrm((N_DENSE, D_MODEL, D_FF), D_MODEL ** -0.5),
        'dense_w_down': nrm((N_DENSE, D_FF, D_MODEL), D_FF ** -0.5),
        'moe_w_router': nrm((N_MOE, D_MODEL, N_EXP), D_MODEL ** -0.5),
        'moe_w_gate': nrm((N_MOE, N_EXP, D_MODEL, D_FF_E), D_MODEL ** -0.5),
        'moe_w_up': nrm((N_MOE, N_EXP, D_MODEL, D_FF_E), D_MODEL ** -0.5),
        'moe_w_down': nrm((N_MOE, N_EXP, D_FF_E, D_MODEL), D_FF_E ** -0.5),
        'norm_final': 1.0 + nrm((D_MODEL,), 0.01),
    }


def reference(x_prompt, x_sample, cache_dsa_k, cache_dsa_v, cache_idx_k, cache_cmp_k, cache_cmp_v,
              cache_slc_k, cache_slc_v, state_win_k, state_win_v, page_table,
              norm_mix, w_in, cmp_pe_k, cmp_w1_k, cmp_w2_k, cmp_pe_v, cmp_w1_v, cmp_w2_v,
              w_up_a, w_up_b, w_o, norm_ffn, dense_w_gate, dense_w_up, dense_w_down,
              moe_w_router, moe_w_gate, moe_w_up, moe_w_down, norm_final):
    S = x_prompt.shape[1]
    T = x_sample.shape[1]
    pos_p = jnp.arange(S)
    pos_s = PAST_LEN + jnp.arange(T)
    names = ('dsa_k', 'dsa_v', 'idx_k', 'cmp_k', 'cmp_v', 'slc_k', 'slc_v', 'win_k', 'win_v')
    newp = {n: [] for n in names}
    news = {n: [] for n in names}
    wp = min(WINDOW, S)
    xp, xs = x_prompt, x_sample
    for l in range(DEPTH):
        cw = (cmp_pe_k[l], cmp_w1_k[l], cmp_w2_k[l], cmp_pe_v[l], cmp_w1_v[l], cmp_w2_v[l])
        ffn_w = (norm_ffn, dense_w_gate, dense_w_up, dense_w_down, moe_w_router, moe_w_gate, moe_w_up, moe_w_down)

        p = _project(_rms(xp, norm_mix[l]), w_in[l], pos_p)
        o_a = _dsa_prompt(p['q_a'], p['k_a'], p['v_a'], p['q_i'], p['w_i'], p['k_i'])
        o_cmp, idx, valid = _nsa_cmp_select(p['q_b'], p['k_cmp'], p['v_cmp'], cw, pos_p, S)
        o_slc = _nsa_slc_prompt(p['q_b'], p['k_slc'], p['v_slc'], idx, valid)
        o_win = _win_prompt(p['q_b'], p['k_win'], p['v_win'])
        xp = _merge(xp, p, o_a, o_cmp, o_slc, o_win, w_up_a[l], w_up_b[l], w_o[l])
        xp = _channel(xp, l, *ffn_w)
        for n, a in zip(names, (p['k_a'], p['v_a'], p['k_i'], p['k_cmp'], p['v_cmp'], p['k_slc'], p['v_slc'],
                                p['k_win'][:, S - wp:], p['v_win'][:, S - wp:])):
            newp[n].append(a)

        q = _project(_rms(xs, norm_mix[l]), w_in[l], pos_s)
        L = PAST_LEN + T
        o_a = _dsa_sample(q['q_a'], q['k_a'], q['v_a'], q['q_i'], q['w_i'], q['k_i'],
                          cache_dsa_k[l], cache_dsa_v[l], cache_idx_k[l], page_table)
        k_raw = jnp.concatenate([_paged_rows(cache_cmp_k[l], page_table), q['k_cmp']], axis=1)
        v_raw = jnp.concatenate([_paged_rows(cache_cmp_v[l], page_table), q['v_cmp']], axis=1)
        o_cmp, idx, valid = _nsa_cmp_select(q['q_b'], k_raw, v_raw, cw, pos_s, L)
        kg = _gather_blocks_sample(cache_slc_k[l], page_table, q['k_slc'], idx)
        vg = _gather_blocks_sample(cache_slc_v[l], page_table, q['v_slc'], idx)
        o_slc = _slc_attend(q['q_b'], kg, vg, idx, valid, pos_s)
        o_win, win_k, win_v = _win_sample(q['q_b'], q['k_win'], q['v_win'], state_win_k[l], state_win_v[l])
        xs = _merge(xs, q, o_a, o_cmp, o_slc, o_win, w_up_a[l], w_up_b[l], w_o[l])
        xs = _channel(xs, l, *ffn_w)
        for n, a in zip(names, (q['k_a'], q['v_a'], q['k_i'], q['k_cmp'], q['v_cmp'], q['k_slc'], q['v_slc'],
                                win_k, win_v)):
            news[n].append(a)

    y_prompt = _rms(xp, norm_final)
    y_sample = _rms(xs, norm_final)
    P = {n: jnp.stack(v, axis=0) for n, v in newp.items()}
    Q = {n: jnp.stack(v, axis=0) for n, v in news.items()}
    return (y_prompt, y_sample,
            P['dsa_k'], P['dsa_v'], P['idx_k'], P['cmp_k'], P['cmp_v'], P['slc_k'], P['slc_v'], P['win_k'], P['win_v'],
            Q['dsa_k'], Q['dsa_v'], Q['idx_k'], Q['cmp_k'], Q['cmp_v'], Q['slc_k'], Q['slc_v'], Q['win_k'], Q['win_v'])
```

```python
import functools

import numpy as np
import jax
import jax.numpy as jnp
from jax import lax
from jax.experimental import pallas as pl
from jax.experimental.pallas import tpu as pltpu

F32 = jnp.float32
BF16 = jnp.bfloat16
I32 = jnp.int32

D_MODEL = 2048
PAST_LEN = 16384
PAGE_SIZE = 128
DH = 128
ROT_DIM = DH // 4
ROPE_THETA = 500000.0
H_A = 8
KVH_A = 2
H_I = 16
D_IDX = 64
ROT_IDX = D_IDX // 4
TOPK_MAX = 256
H_B = 8
KVH_B = 2
G = 4
L_CMP = 32
S_CMP = 16
L_SLC = 64
N_SEL_MAX = 16
WINDOW = 512
N_EXP = 8
Q_BLK = 128
EPS = 1e-6
NEG = -1e30
FORCED = 1e6
INT_MIN = -2 ** 31
SCALE = DH ** -0.5
PAGES_PER_STEP = 16
SAMPLE_ROWS = 16
VMEM_LIMIT = 48 * 1024 * 1024

_IN_SPLITS = (
    ('q_a', H_A * DH), ('k_a', KVH_A * DH), ('v_a', KVH_A * DH),
    ('q_i', H_I * D_IDX), ('k_i', D_IDX), ('w_i', H_I),
    ('q_b', H_B * DH),
    ('k_cmp', KVH_B * DH), ('v_cmp', KVH_B * DH),
    ('k_slc', KVH_B * DH), ('v_slc', KVH_B * DH),
    ('k_win', KVH_B * DH), ('v_win', KVH_B * DH),
    ('g_b', 3 * H_B), ('g_mix', 2 * D_MODEL),
)
_GROUP_ROPE128 = ('q_a', 'q_b', 'k_a', 'k_cmp', 'k_slc', 'k_win')
_GROUP_ROPE64 = ('q_i', 'k_i')
_GROUP_PLAIN = ('v_a', 'v_cmp', 'v_slc', 'v_win', 'w_i', 'g_b')
W_R64 = 1280
W_PLAIN = 1280
GATE_OFF = H_I


def _params(*sem):
    return pltpu.CompilerParams(dimension_semantics=sem, vmem_limit_bytes=VMEM_LIMIT)


def _dot(a, b):
    return jnp.dot(a.astype(BF16), b.astype(BF16), preferred_element_type=F32)


def _dg_nt(a, b):
    return lax.dot_general(a, b, (((1,), (1,)), ((), ())), preferred_element_type=F32)


def _dot_nt(a, b):
    return _dg_nt(a.astype(BF16), b.astype(BF16))


def _split(a):
    hi = a.astype(BF16)
    lo = (a - hi.astype(F32)).astype(BF16)
    return hi, lo


def _dot3_nt(a, b):
    ah, al = _split(a)
    bh, bl = _split(b)
    return _dg_nt(ah, bh) + _dg_nt(ah, bl) + _dg_nt(al, bh)


def _mprobs(s, mask):
    s = jnp.where(mask, s, NEG)
    m = jnp.max(s, axis=-1, keepdims=True)
    p = jnp.where(mask, jnp.exp(s - m), 0.0)
    return p * (1.0 / jnp.maximum(jnp.sum(p, axis=-1, keepdims=True), 1e-30))


def _order_key(x):
    b = lax.bitcast_convert_type(x + 0.0, I32)
    return jnp.where(b < 0, b ^ jnp.int32(0x7FFFFFFF), b)


def _kth_largest_key(count_ge, rows, k):
    def body(it, lo):
        cand = lo + jnp.left_shift(jnp.int32(1), 31 - it)
        return jnp.where(count_ge(cand) >= k, cand, lo)
    return lax.fori_loop(0, 32, body, jnp.full((rows, 1), INT_MIN, I32))


def _rms_kernel(x_ref, g_ref, o_ref):
    x = x_ref[...]
    y = x * lax.rsqrt(jnp.mean(x * x, axis=-1, keepdims=True) + EPS)
    o_ref[...] = (y * g_ref[...]).astype(o_ref.dtype)


def _rms(x, g, out_dtype):
    M, D = x.shape
    tm = min(M, 512)
    return pl.pallas_call(
        _rms_kernel, grid=(M // tm,), name='rms',
        in_specs=[pl.BlockSpec((tm, D), lambda i: (i, 0)), pl.BlockSpec((1, D), lambda i: (0, 0))],
        out_specs=pl.BlockSpec((tm, D), lambda i: (i, 0)),
        out_shape=jax.ShapeDtypeStruct((M, D), out_dtype),
        compiler_params=_params('parallel'),
    )(x, g.reshape(1, D))


def _mm_kernel(*refs, mode, shift):
    a_ref, b_ref = refs[0], refs[1]
    o_ref = refs[-1]
    acc = jnp.dot(a_ref[...], b_ref[...], preferred_element_type=F32)
    if mode == 'rope':
        c_ref, s1_ref, s2_ref = refs[2:5]
        tn = acc.shape[1]
        rep = tn // 128
        til = lambda r: jnp.concatenate([r[...]] * rep, axis=1)
        acc = (acc * til(c_ref) + pltpu.roll(acc, tn - shift, 1) * til(s1_ref)
               + pltpu.roll(acc, shift, 1) * til(s2_ref))
    elif mode == 'sigmoid':
        acc = jax.nn.sigmoid(acc)
    elif mode == 'res':
        acc = acc + refs[2][...]
    o_ref[...] = acc.astype(o_ref.dtype)


def _mm(a, b, tn, mode='none', shift=0, extra=(), out_dtype=F32, name='mm'):
    M, K = a.shape
    N = b.shape[1]
    tm = min(M, 1024)
    in_specs = [pl.BlockSpec((tm, K), lambda i, j: (i, 0)), pl.BlockSpec((K, tn), lambda i, j: (0, j))]
    if mode == 'rope':
        in_specs += [pl.BlockSpec((tm, 128), lambda i, j: (i, 0))] * 3
    elif mode == 'res':
        in_specs += [pl.BlockSpec((tm, tn), lambda i, j: (i, j))]
    return pl.pallas_call(
        functools.partial(_mm_kernel, mode=mode, shift=shift), grid=(M // tm, N // tn), name=name,
        in_specs=in_specs, out_specs=pl.BlockSpec((tm, tn), lambda i, j: (i, j)),
        out_shape=jax.ShapeDtypeStruct((M, N), out_dtype),
        compiler_params=_params('parallel', 'parallel'),
    )(a, b, *extra)


def _rope_tables(pos, rot, period):
    half = rot // 2
    inv = jnp.power(jnp.float32(ROPE_THETA), -jnp.arange(half, dtype=F32) / half)
    ang = pos.astype(F32)[:, None] * inv[None, :]
    cos, sin = jnp.cos(ang), jnp.sin(ang)
    T = pos.shape[0]
    z = jnp.zeros((T, period - rot), F32)
    zh = jnp.zeros((T, half), F32)
    c = jnp.concatenate([cos, cos, jnp.ones((T, period - rot), F32)], axis=1)
    s1 = jnp.concatenate([-sin, zh, z], axis=1)
    s2 = jnp.concatenate([zh, sin, z], axis=1)
    rep = 128 // period
    return tuple(jnp.tile(t, (1, rep)) for t in (c, s1, s2))


def _split_w_in(w):
    parts, off = {}, 0
    for name, width in _IN_SPLITS:
        parts[name] = w[:, off:off + width]
        off += width
    cat = lambda names, total: jnp.pad(
        jnp.concatenate([parts[n] for n in names], axis=1),
        ((0, 0), (0, total - sum(parts[n].shape[1] for n in names)))).astype(BF16)
    return (cat(_GROUP_ROPE128, 3072), cat(_GROUP_ROPE64, W_R64), cat(_GROUP_PLAIN, W_PLAIN),
            parts['g_mix'].astype(BF16))


def _project(h, wsplit, tabs128, tabs64):
    w128, w64, wpl, wg = wsplit
    z128 = _mm(h, w128, 512, 'rope', ROT_DIM // 2, tabs128, name='proj_rope128')
    z64 = _mm(h, w64, 640, 'rope', ROT_IDX // 2, tabs64, name='proj_rope64')
    zpl = _mm(h, wpl, 640, name='proj_plain')
    zg = _mm(h, wg, 512, 'sigmoid', name='proj_gmix')
    return z128, z64, zpl, zg


def _dsa_prompt_kernel(qa_ref, ka_ref, va_ref, qi_ref, ki_ref, wi_ref, o_ref, *, k_sel):
    i = pl.program_id(1)
    S = ka_ref.shape[1]
    ki = ki_ref[0][:, :D_IDX]
    w = wi_ref[0][:, :H_I] * (H_I ** -0.5)
    score = jnp.zeros((Q_BLK, S), F32)
    for h in range(H_I):
        d = _dot3_nt(qi_ref[0, :, h * D_IDX:(h + 1) * D_IDX], ki) * (D_IDX ** -0.5)
        score = score + jnp.maximum(d, 0.0) * w[:, h:h + 1]
    qpos = i * Q_BLK + lax.broadcasted_iota(I32, (Q_BLK, 1), 0)
    kpos = lax.broadcasted_iota(I32, (1, S), 1)
    masked = jnp.where(kpos <= qpos, score, NEG)
    key = _order_key(masked)
    count_ge = lambda c: jnp.sum(jnp.where(key >= c, 1.0, 0.0), axis=1, keepdims=True)
    thr = _kth_largest_key(count_ge, Q_BLK, k_sel)
    sel = jnp.where(masked > NEG * 0.5, key, INT_MIN) >= jnp.maximum(thr, INT_MIN + 1)
    for kv in range(KVH_A):
        k = ka_ref[0, :, kv * DH:(kv + 1) * DH].astype(BF16)
        v = va_ref[0, :, kv * DH:(kv + 1) * DH].astype(BF16)
        for g in range(G):
            hd = kv * G + g
            s = _dot_nt(qa_ref[0, :, hd * DH:(hd + 1) * DH], k) * SCALE
            o_ref[0, :, hd * DH:(hd + 1) * DH] = _dot(_mprobs(s, sel), v)


def _dsa_prompt(z128, z64, zpl, B, S):
    k_sel = min(TOPK_MAX, S // 4)
    blk = lambda w, f: pl.BlockSpec((1, Q_BLK, w), f)
    full = lambda w, c: pl.BlockSpec((1, S, w), lambda b, i: (b, 0, c))
    return pl.pallas_call(
        functools.partial(_dsa_prompt_kernel, k_sel=k_sel), grid=(B, S // Q_BLK), name='dsa_prompt',
        in_specs=[blk(1024, lambda b, i: (b, i, 0)), full(256, 8), full(256, 0),
                  blk(1024, lambda b, i: (b, i, 0)), full(128, 8), blk(128, lambda b, i: (b, i, 8))],
        out_specs=blk(1024, lambda b, i: (b, i, 0)),
        out_shape=jax.ShapeDtypeStruct((B, S, H_A * DH), F32),
        compiler_params=_params('parallel', 'parallel'),
    )(z128, z128, zpl, z64, z64, zpl)


def _cmp_partial_kernel(*refs, n_src):
    refs = refs[len(refs) - (2 * n_src + 8):]
    rw = KVH_B * DH
    ksrc, vsrc = refs[:n_src], refs[n_src:2 * n_src]
    pek, w1k, pev, w1v, ak, bk, av, bv = refs[2 * n_src:]
    half = L_CMP // 2
    for srcs, pe_ref, w1_ref, a_ref, b_ref in ((ksrc, pek, w1k, ak, bk), (vsrc, pev, w1v, av, bv)):
        lead = (0,) * (len(srcs[0].shape) - 2)
        for kvh in range(KVH_B):
            acc_a = acc_b = None
            for p in range(half):
                rows = jnp.concatenate(
                    [r[lead + (slice(None), slice(p * rw + kvh * DH, p * rw + (kvh + 1) * DH))] for r in srcs], axis=0)
                da = _dot(rows + pe_ref[p:p + 1, :], w1_ref[p * DH:(p + 1) * DH, :])
                db = _dot(rows + pe_ref[half + p:half + p + 1, :], w1_ref[(half + p) * DH:(half + p + 1) * DH, :])
                acc_a = da if acc_a is None else acc_a + da
                acc_b = db if acc_b is None else acc_b + db
            a_ref[0, :, kvh * DH:(kvh + 1) * DH] = acc_a
            b_ref[0, :, kvh * DH:(kvh + 1) * DH] = acc_b


def _cmp_finish_kernel(ak_ref, bk_ref, av_ref, bv_ref, w2k_ref, w2v_ref, kc_ref, vc_ref):
    n = ak_ref.shape[1]
    for a_ref, b_ref, w2_ref, o_ref in ((ak_ref, bk_ref, w2k_ref, kc_ref), (av_ref, bv_ref, w2v_ref, vc_ref)):
        for kvh in range(KVH_B):
            sl = slice(kvh * DH, (kvh + 1) * DH)
            h = a_ref[0, :, sl] + pltpu.roll(b_ref[0, :, sl], n - 1, 0)
            o_ref[0, :, sl] = _dot(jax.nn.gelu(h), w2_ref[...])


def _cmp_finish(parts, w2k, w2v):
    Bt, n, _ = parts[0].shape
    spec = pl.BlockSpec((1, n, KVH_B * DH), lambda b: (b, 0, 0))
    wspec = pl.BlockSpec((DH, DH), lambda b: (0, 0))
    return pl.pallas_call(
        _cmp_finish_kernel, grid=(Bt,), name='cmp_finish',
        in_specs=[spec] * 4 + [wspec] * 2, out_specs=[spec] * 2,
        out_shape=[jax.ShapeDtypeStruct((Bt, n, KVH_B * DH), F32)] * 2,
        compiler_params=_params('parallel'),
    )(*parts, w2k.astype(BF16), w2v.astype(BF16))


def _cmp_weight_specs(imap):
    return [pl.BlockSpec((L_CMP, DH), imap), pl.BlockSpec((L_CMP * DH, DH), imap)] * 2


def _compress_prompt(z128, zpl, cw, B, S):
    pek, w1k, w2k, pev, w1v, w2v = cw
    nch = S // S_CMP
    const = lambda b: (0, 0)
    out_spec = pl.BlockSpec((1, nch, KVH_B * DH), lambda b: (b, 0, 0))
    src_spec = pl.BlockSpec((1, nch, S_CMP * 256), lambda b: (b, 0, 0))
    chunked = lambda z, c: z[:, :, c:c + 256].reshape(B, nch, S_CMP * 256)
    parts = pl.pallas_call(
        functools.partial(_cmp_partial_kernel, n_src=1), grid=(B,), name='cmp_partial_prompt',
        in_specs=[src_spec, src_spec] + _cmp_weight_specs(const),
        out_specs=[out_spec] * 4,
        out_shape=[jax.ShapeDtypeStruct((B, nch, KVH_B * DH), F32)] * 4,
        compiler_params=_params('parallel'),
    )(chunked(z128, 2304), chunked(zpl, 256), pek, w1k.astype(BF16), pev, w1v.astype(BF16))
    return _cmp_finish(parts, w2k, w2v)


def _nsa_prompt_kernel(qb_ref, kc_ref, vc_ref, ks_ref, vs_ref, kw_ref, vw_ref, gt_ref, ov_ref, ex_ref, o_ref,
                       *, n_cmp, n_slc, n_sel):
    i = pl.program_id(1)
    S = ks_ref.shape[1]
    wspan = WINDOW + Q_BLK
    qpos = i * Q_BLK + lax.broadcasted_iota(I32, (Q_BLK, 1), 0)
    lane = lax.broadcasted_iota(I32, (1, 128), 1)
    kpos = lax.broadcasted_iota(I32, (1, S), 1)
    causal = kpos <= qpos
    vis = (lane * S_CMP + (L_CMP - 1) <= qpos) & (lane < n_cmp)
    jt = qpos // L_SLC
    forced = (lane == 0) | (lane == jt) | (lane == jt - 1)
    wstart = pl.multiple_of(jnp.clip(i * Q_BLK - WINDOW, 0, S - wspan), Q_BLK)
    wpos = wstart + lax.broadcasted_iota(I32, (1, wspan), 1)
    wmask = (wpos <= qpos) & (wpos > qpos - WINDOW)
    gates = jax.nn.sigmoid(gt_ref[0])
    for kv in range(KVH_B):
        sl = slice(kv * DH, (kv + 1) * DH)
        kc = kc_ref[0, :, sl].astype(BF16)
        vc = vc_ref[0, :, sl].astype(BF16)
        qs = [qb_ref[0, :, (kv * G + g) * DH:(kv * G + g + 1) * DH].astype(BF16) for g in range(G)]
        o_cmp, psum = [], None
        for g in range(G):
            pr = _mprobs(_dg_nt(qs[g], kc) * SCALE, vis)
            o_cmp.append(_dot(pr, vc))
            psum = pr if psum is None else psum + pr
        ph, plo = _split(psum)
        imp = jnp.dot(ph, ov_ref[...], preferred_element_type=F32) + jnp.dot(plo, ov_ref[...], preferred_element_type=F32)
        sc = jnp.where(lane > jt, NEG, jnp.where(forced, FORCED, imp))
        rank = jnp.zeros((Q_BLK, 128), F32)
        for j2 in range(n_slc):
            col = sc[:, j2:j2 + 1]
            rank = rank + jnp.where(col > sc, 1.0, jnp.where((col == sc) & (lane > j2), 1.0, 0.0))
        sel = jnp.where((rank < n_sel) & (sc > NEG * 0.5), 1.0, 0.0).astype(BF16)
        tokmask = (jnp.dot(sel, ex_ref[...], preferred_element_type=F32) > 0.5) & causal
        ks = ks_ref[0, :, sl].astype(BF16)
        vs = vs_ref[0, :, sl].astype(BF16)
        kw = kw_ref[0, pl.ds(wstart, wspan), sl].astype(BF16)
        vw = vw_ref[0, pl.ds(wstart, wspan), sl].astype(BF16)
        for g in range(G):
            hd = kv * G + g
            o_slc = _dot(_mprobs(_dg_nt(qs[g], ks) * SCALE, tokmask), vs)
            o_win = _dot(_mprobs(_dg_nt(qs[g], kw) * SCALE, wmask), vw)
            c0 = GATE_OFF + hd * 3
            o_ref[0, :, hd * DH:(hd + 1) * DH] = (gates[:, c0:c0 + 1] * o_cmp[g] + gates[:, c0 + 1:c0 + 2] * o_slc
                                                  + gates[:, c0 + 2:c0 + 3] * o_win)


def _overlap_np(nc, nslc, rows, cols):
    cs = np.arange(rows)[:, None] * S_CMP
    js = np.arange(cols)[None, :] * L_SLC
    ov = (cs < js + L_SLC) & (cs + L_CMP > js) & (np.arange(rows)[:, None] < nc) & (np.arange(cols)[None, :] < nslc)
    return jnp.asarray(ov.astype(np.float32), dtype=BF16)


def _nsa_prompt(z128, zpl, kc, vc, B, S):
    n_cmp = S // S_CMP - L_CMP // S_CMP + 1
    n_slc = -(-S // L_SLC)
    assert n_cmp <= 128 and n_slc <= 128 and kc.shape[1] == 128
    ov = _overlap_np(n_cmp, n_slc, 128, 128)
    ex = jnp.asarray((np.arange(128)[:, None] == np.arange(S)[None, :] // L_SLC).astype(np.float32), dtype=BF16)
    blk = lambda w, f: pl.BlockSpec((1, Q_BLK, w), f)
    full = lambda w, c: pl.BlockSpec((1, S, w), lambda b, i: (b, 0, c))
    const = lambda shape: pl.BlockSpec(shape, lambda b, i: (0, 0))
    cspec = pl.BlockSpec((1, 128, 256), lambda b, i: (b, 0, 0))
    return pl.pallas_call(
        functools.partial(_nsa_prompt_kernel, n_cmp=n_cmp, n_slc=n_slc, n_sel=min(N_SEL_MAX, n_slc)),
        grid=(B, S // Q_BLK), name='nsa_prompt',
        in_specs=[blk(1024, lambda b, i: (b, i, 1)), cspec, cspec, full(256, 10), full(256, 2), full(256, 11), full(256, 3),
                  blk(128, lambda b, i: (b, i, 8)), const((128, 128)), const((128, S))],
        out_specs=blk(1024, lambda b, i: (b, i, 0)),
        out_shape=jax.ShapeDtypeStruct((B, S, H_B * DH), F32),
        compiler_params=_params('parallel', 'parallel'),
    )(z128, kc, vc, z128, zpl, z128, zpl, zpl, ov, ex)


def _merge_u_kernel(oa_ref, ob_ref, wa_ref, wb_ref, ga_ref, gb_ref, o_ref):
    o_ref[...] = (ga_ref[...] * _dot(oa_ref[...], wa_ref[...]) + gb_ref[...] * _dot(ob_ref[...], wb_ref[...])).astype(o_ref.dtype)


def _merge(x, o_a, o_b, zg, wa, wb, wo):
    M = x.shape[0]
    tm, tn = min(M, 512), 512
    nj = D_MODEL // tn
    u = pl.pallas_call(
        _merge_u_kernel, grid=(M // tm, nj), name='merge_u',
        in_specs=[pl.BlockSpec((tm, H_A * DH), lambda i, j: (i, 0)), pl.BlockSpec((tm, H_B * DH), lambda i, j: (i, 0)),
                  pl.BlockSpec((H_A * DH, tn), lambda i, j: (0, j)), pl.BlockSpec((H_B * DH, tn), lambda i, j: (0, j)),
                  pl.BlockSpec((tm, tn), lambda i, j: (i, j)), pl.BlockSpec((tm, tn), lambda i, j: (i, j + nj))],
        out_specs=pl.BlockSpec((tm, tn), lambda i, j: (i, j)),
        out_shape=jax.ShapeDtypeStruct((M, D_MODEL), BF16),
        compiler_params=_params('parallel', 'parallel'),
    )(o_a, o_b, wa, wb, zg, zg)
    return _mm(u, wo, 512, 'res', extra=(x,), name='merge_out')


def _ffn_kernel(te_ref, nu_ref, h_ref, wg_ref, wu_ref, wd_ref, *rest, has_res):
    o_ref, acc_ref = rest[-2], rest[-1]
    m, f = pl.program_id(0), pl.program_id(1)

    @pl.when(f == 0)
    def _():
        acc_ref[...] = jnp.zeros_like(acc_ref)

    @pl.when(m < nu_ref[0])
    def _():
        h = h_ref[...]
        a = jnp.dot(h, wg_ref[0], preferred_element_type=F32)
        b = jnp.dot(h, wu_ref[0], preferred_element_type=F32)
        acc_ref[...] += _dot(a * jax.nn.sigmoid(a) * b, wd_ref[0])

    @pl.when(f == pl.num_programs(1) - 1)
    def _():
        o_ref[...] = acc_ref[...] + rest[0][...] if has_res else acc_ref[...]


def _ffn(h, wg, wu, wd, tile_expert, n_used, res=None):
    R, D = h.shape
    F = wg.shape[2]
    tm, tf = min(R, 512), 512
    nf = F // tf
    fe = lambda m, f, nu: jnp.where(m < nu[0], f, nf - 1)
    in_specs = [pl.BlockSpec((tm, D), lambda m, f, te, nu: (m, 0)),
                pl.BlockSpec((1, D, tf), lambda m, f, te, nu: (te[m], 0, fe(m, f, nu))),
                pl.BlockSpec((1, D, tf), lambda m, f, te, nu: (te[m], 0, fe(m, f, nu))),
                pl.BlockSpec((1, tf, D), lambda m, f, te, nu: (te[m], fe(m, f, nu), 0))]
    args = [h, wg, wu, wd]
    if res is not None:
        in_specs.append(pl.BlockSpec((tm, D), lambda m, f, te, nu: (m, 0)))
        args.append(res)
    return pl.pallas_call(
        functools.partial(_ffn_kernel, has_res=res is not None), name='ffn',
        grid_spec=pltpu.PrefetchScalarGridSpec(
            num_scalar_prefetch=2, grid=(R // tm, nf), in_specs=in_specs,
            out_specs=pl.BlockSpec((tm, D), lambda m, f, te, nu: (m, 0)),
            scratch_shapes=[pltpu.VMEM((tm, D), F32)]),
        out_shape=jax.ShapeDtypeStruct((R, D), F32),
        compiler_params=_params('parallel', 'arbitrary'),
    )(tile_expert, n_used, *args)


def _dense_ffn(x, g, wg, wu, wd):
    h = _rms(x, g, BF16)
    n_tiles = x.shape[0] // min(x.shape[0], 512)
    return _ffn(h, wg[None], wu[None], wd[None], jnp.zeros((n_tiles,), I32), jnp.full((1,), n_tiles, I32), res=x)


def _router_kernel(x_ref, g_ref, wh_ref, wl_ref, h_ref, idx_ref, gate_ref):
    x = x_ref[...]
    h = x * lax.rsqrt(jnp.mean(x * x, axis=-1, keepdims=True) + EPS) * g_ref[...]
    h_ref[...] = h.astype(BF16)
    hh, hl = _split(h)
    d = lambda a, b: jnp.dot(a, b, preferred_element_type=F32)
    logits = d(hh, wh_ref[...]) + d(hh, wl_ref[...]) + d(hl, wh_ref[...])
    lane = lax.broadcasted_iota(I32, logits.shape, 1)
    lanef = lane.astype(F32)
    l1 = jnp.where(lane < N_EXP, logits, -jnp.inf)
    m1 = jnp.max(l1, axis=1, keepdims=True)
    i1 = jnp.min(jnp.where(l1 == m1, lanef, 128.0), axis=1, keepdims=True)
    l2 = jnp.where(lanef == i1, -jnp.inf, l1)
    m2 = jnp.max(l2, axis=1, keepdims=True)
    i2 = jnp.min(jnp.where(l2 == m2, lanef, 128.0), axis=1, keepdims=True)
    e = jnp.exp(m2 - m1)
    inv = 1.0 / (1.0 + e)
    idx_ref[...] = jnp.where(lane == 0, i1, i2).astype(I32)
    gate_ref[...] = jnp.where(lane == 0, inv, e * inv)


def _router(x, g, w_r):
    M, D = x.shape
    tm = min(M, 512)
    wpad = jnp.pad(w_r, ((0, 0), (0, 128 - N_EXP)))
    wh = wpad.astype(BF16)
    wl = (wpad - wh.astype(F32)).astype(BF16)
    row = lambda w, dt: (pl.BlockSpec((tm, w), lambda i: (i, 0)), jax.ShapeDtypeStruct((M, w), dt))
    outs = [row(D, BF16), row(128, I32), row(128, F32)]
    h, idx, gate = pl.pallas_call(
        _router_kernel, grid=(M // tm,), name='router',
        in_specs=[pl.BlockSpec((tm, D), lambda i: (i, 0)), pl.BlockSpec((1, D), lambda i: (0, 0)),
                  pl.BlockSpec((D, 128), lambda i: (0, 0)), pl.BlockSpec((D, 128), lambda i: (0, 0))],
        out_specs=[o[0] for o in outs], out_shape=[o[1] for o in outs],
        compiler_params=_params('parallel'),
    )(x, g.reshape(1, D), wh, wl)
    return h, idx[:, :2], gate[:, :2]


def _moe(xp, xs, g, w_r, wg, wu, wd):
    hp, ip, gp = _router(xp, g, w_r)
    hs, is_, gs = _router(xs, g, w_r)
    h = jnp.concatenate([hp, hs], axis=0)
    idx = jnp.concatenate([ip, is_], axis=0)
    gate = jnp.concatenate([gp, gs], axis=0)
    M = h.shape[0]
    tm = 512
    n_tiles = (2 * M + N_EXP * (tm - 1) + tm - 1) // tm
    e = jnp.concatenate([idx[:, 0], idx[:, 1]])
    order = jnp.argsort(e, stable=True).astype(I32)
    counts = jnp.sum((e[:, None] == jnp.arange(N_EXP, dtype=I32)[None, :]).astype(I32), axis=0)
    tiles_per = (counts + tm - 1) // tm
    tile_end = jnp.cumsum(tiles_per)
    row_start = (tile_end - tiles_per) * tm
    grp_start = jnp.cumsum(counts) - counts
    e_sorted = e[order]
    dest = row_start[e_sorted] + jnp.arange(2 * M, dtype=I32) - grp_start[e_sorted]
    src_tok = jnp.zeros((n_tiles * tm,), I32).at[dest].set(order % M)
    pos = jnp.zeros((2 * M,), I32).at[order].set(dest)
    n_used = tile_end[-1].astype(I32)
    t = jnp.arange(n_tiles, dtype=I32)
    tile_expert = jnp.searchsorted(tile_end, jnp.minimum(t, n_used - 1), side='right').astype(I32)
    y = _ffn(jnp.take(h, src_tok, axis=0), wg, wu, wd, tile_expert, n_used.reshape(1))
    f = gate[:, 0:1] * jnp.take(y, pos[:M], axis=0) + gate[:, 1:2] * jnp.take(y, pos[M:], axis=0)
    Mp = xp.shape[0]
    return xp + f[:Mp], xs + f[Mp:]


def _page_specs(block, n_lead, layer, col=None):
    def spec(j):
        def imap(b, s, pt):
            page = pt[b * (PAST_LEN // PAGE_SIZE) + s * PAGES_PER_STEP + j]
            return (layer, page) + (0,) * n_lead
        return pl.BlockSpec(block, imap)
    return [spec(j) for j in range(PAGES_PER_STEP)]


def _idx_walk_kernel(pt_ref, q_ref, w_ref, *refs):
    pages, o_ref = refs[:-1], refs[-1]
    keys = jnp.concatenate([r[0, 0] for r in pages], axis=0)
    d = _dot3_nt(q_ref[0], keys) * (D_IDX ** -0.5)
    o_ref[0, 0] = jnp.sum(jnp.maximum(d, 0.0) * (w_ref[0] * (H_I ** -0.5)), axis=0, keepdims=True)


def _dsa_thresh_kernel(sc_ref, qi_ref, wi_ref, kin_ref, sel_ref, selnew_ref, *, k_sel):
    nb = sc_ref.shape[0]
    snew = []
    for b in range(nb):
        d = jnp.sum(qi_ref[b] * kin_ref[b:b + 1, :], axis=1, keepdims=True) * (D_IDX ** -0.5)
        snew.append(jnp.sum(jnp.maximum(d, 0.0) * (wi_ref[b] * (H_I ** -0.5)), axis=0, keepdims=True))
    snew = jnp.concatenate(snew, axis=0)
    key, knew = _order_key(sc_ref[...]), _order_key(snew)
    count_ge = lambda c: (jnp.sum(jnp.where(key >= c, 1.0, 0.0), axis=1, keepdims=True) + jnp.where(knew >= c, 1.0, 0.0))
    thr = _kth_largest_key(count_ge, nb, k_sel)
    sel_ref[...] = jnp.where((key >= thr) & (sc_ref[...] > NEG * 0.5), 1.0, 0.0)
    selnew_ref[...] = jnp.broadcast_to(jnp.where((knew >= thr) & (snew > NEG * 0.5), 1.0, 0.0), selnew_ref.shape)


def _dsa_walk_kernel(pt_ref, q_ref, sel_ref, selnew_ref, knew_ref, vnew_ref, *refs):
    n = PAGES_PER_STEP
    kpages, vpages = refs[:n], refs[n:2 * n]
    o_ref, m_ref, l_ref, acc_ref = refs[2 * n:]
    s_id = pl.program_id(1)
    row = lax.broadcasted_iota(I32, (H_A, 1), 0)
    first = row < G

    @pl.when(s_id == 0)
    def _():
        m_ref[...] = jnp.full_like(m_ref, NEG)
        l_ref[...] = jnp.zeros_like(l_ref)
        acc_ref[...] = jnp.zeros_like(acc_ref)

    q = q_ref[0]
    k = jnp.concatenate([r[0, 0] for r in kpages], axis=0).astype(BF16)
    v = jnp.concatenate([r[0, 0] for r in vpages], axis=0).astype(BF16)
    s = jnp.where(first, _dot_nt(q, k[:, :DH]), _dot_nt(q, k[:, DH:])) * SCALE
    mask = sel_ref[0, 0] > 0.5
    s = jnp.where(mask, s, NEG)
    m_new = jnp.maximum(m_ref[...], jnp.max(s, axis=1, keepdims=True))
    p = jnp.where(mask, jnp.exp(s - m_new), 0.0)
    alpha = jnp.exp(m_ref[...] - m_new)
    pb = p.astype(BF16)
    pv = jnp.where(first, jnp.dot(pb, v[:, :DH], preferred_element_type=F32), jnp.dot(pb, v[:, DH:], preferred_element_type=F32))
    l_ref[...] = alpha * l_ref[...] + jnp.sum(p, axis=1, keepdims=True)
    acc_ref[...] = alpha * acc_ref[...] + pv
    m_ref[...] = m_new

    @pl.when(s_id == pl.num_programs(1) - 1)
    def _():
        kn = jnp.where(first, knew_ref[0][:, :DH], knew_ref[0][:, DH:])
        vn = jnp.where(first, vnew_ref[0][:, :DH], vnew_ref[0][:, DH:])
        valid = selnew_ref[0][:, :1] > 0.5
        sn = jnp.where(valid, jnp.sum(q * kn, axis=1, keepdims=True) * SCALE, NEG)
        m_fin = jnp.maximum(m_ref[...], sn)
        pn = jnp.where(valid, jnp.exp(sn - m_fin), 0.0)
        a2 = jnp.exp(m_ref[...] - m_fin)
        l_fin = a2 * l_ref[...] + pn
        o_ref[0] = (a2 * acc_ref[...] + pn * vn) * (1.0 / jnp.maximum(l_fin, 1e-30))


def _dsa_sample(l, z128, z64, zpl, cache_k, cache_v, cache_ki, pt_flat, DB):
    n_pages = PAST_LEN // PAGE_SIZE
    n_steps = n_pages // PAGES_PER_STEP
    span = PAGES_PER_STEP * PAGE_SIZE
    k_sel = min(TOPK_MAX, (PAST_LEN + 1) // 4)
    qi = z64[:DB, :H_I * D_IDX].reshape(DB, H_I, D_IDX)
    wi = zpl[:DB, 1024:1024 + H_I].reshape(DB, H_I, 1)
    kin = z64[:DB, 1024:1024 + D_IDX]
    scores = pl.pallas_call(
        _idx_walk_kernel, name='idx_walk',
        grid_spec=pltpu.PrefetchScalarGridSpec(
            num_scalar_prefetch=1, grid=(DB, n_steps),
            in_specs=[pl.BlockSpec((1, H_I, D_IDX), lambda b, s, pt: (b, 0, 0)), pl.BlockSpec((1, H_I, 1), lambda b, s, pt: (b, 0, 0))]
            + _page_specs((1, 1, PAGE_SIZE, D_IDX), 2, l),
            out_specs=pl.BlockSpec((1, 1, 1, span), lambda b, s, pt: (b, s, 0, 0))),
        out_shape=jax.ShapeDtypeStruct((DB, n_steps, 1, span), F32),
        compiler_params=_params('parallel', 'parallel'),
    )(pt_flat, qi, wi, *([cache_ki] * PAGES_PER_STEP))
    sel, selnew = pl.pallas_call(
        functools.partial(_dsa_thresh_kernel, k_sel=k_sel), name='dsa_thresh',
        out_shape=[jax.ShapeDtypeStruct((DB, PAST_LEN), F32), jax.ShapeDtypeStruct((DB, 128), F32)],
        compiler_params=pltpu.CompilerParams(vmem_limit_bytes=VMEM_LIMIT),
    )(scores.reshape(DB, PAST_LEN), qi, wi, kin)
    qa = z128[:DB, :H_A * DH].reshape(DB, H_A, DH)
    knew = jnp.broadcast_to(z128[:DB, None, 2048:2304], (DB, H_A, KVH_A * DH))
    vnew = jnp.broadcast_to(zpl[:DB, None, 0:256], (DB, H_A, KVH_A * DH))
    selnew = jnp.broadcast_to(selnew[:, None, :], (DB, H_A, 128))
    per_b = lambda shape: pl.BlockSpec(shape, lambda b, s, pt: (b, 0, 0))
    o = pl.pallas_call(
        _dsa_walk_kernel, name='dsa_walk',
        grid_spec=pltpu.PrefetchScalarGridSpec(
            num_scalar_prefetch=1, grid=(DB, n_steps),
            in_specs=[per_b((1, H_A, DH)), pl.BlockSpec((1, 1, 1, span), lambda b, s, pt: (b, s, 0, 0)),
                      per_b((1, H_A, 128)), per_b((1, H_A, KVH_A * DH)), per_b((1, H_A, KVH_A * DH))]
            + _page_specs((1, 1, PAGE_SIZE, KVH_A * DH), 2, l) * 1 + _page_specs((1, 1, PAGE_SIZE, KVH_A * DH), 2, l),
            out_specs=per_b((1, H_A, DH)),
            scratch_shapes=[pltpu.VMEM((H_A, 1), F32), pltpu.VMEM((H_A, 1), F32), pltpu.VMEM((H_A, DH), F32)]),
        out_shape=jax.ShapeDtypeStruct((DB, H_A, DH), F32),
        compiler_params=_params('parallel', 'arbitrary'),
    )(pt_flat, qa, sel.reshape(DB, n_steps, 1, span), selnew, knew, vnew,
      *([cache_k] * PAGES_PER_STEP), *([cache_v] * PAGES_PER_STEP))
    return o.reshape(DB, H_A * DH)


def _compress_sample(l, cache_k, cache_v, cw, pt_flat, DB):
    pek, w1k, w2k, pev, w1v, w2v = cw
    n_pages = PAST_LEN // PAGE_SIZE
    n_steps = n_pages // PAGES_PER_STEP
    cps = PAGE_SIZE // S_CMP
    nch_step = PAGES_PER_STEP * cps
    const = lambda b, s, pt: (0, 0)
    out_spec = pl.BlockSpec((1, nch_step, KVH_B * DH), lambda b, s, pt: (b, s, 0))
    page_specs = _page_specs((1, 1, cps, S_CMP * 256), 2, l)
    chunked = lambda c: c.reshape(c.shape[0], c.shape[1], cps, S_CMP * 256)
    parts = pl.pallas_call(
        functools.partial(_cmp_partial_kernel, n_src=PAGES_PER_STEP), name='cmp_partial_sample',
        grid_spec=pltpu.PrefetchScalarGridSpec(
            num_scalar_prefetch=1, grid=(DB, n_steps),
            in_specs=page_specs + page_specs + _cmp_weight_specs(const),
            out_specs=[out_spec] * 4),
        out_shape=[jax.ShapeDtypeStruct((DB, PAST_LEN // S_CMP, KVH_B * DH), F32)] * 4,
        compiler_params=_params('parallel', 'parallel'),
    )(pt_flat, *([chunked(cache_k)] * PAGES_PER_STEP), *([chunked(cache_v)] * PAGES_PER_STEP),
      pek, w1k.astype(BF16), pev, w1v.astype(BF16))
    return _cmp_finish(parts, w2k, w2v)


def _cmp_select_sample_kernel(q_ref, kc_ref, vc_ref, ov_ref, o_ref, ids_ref, *, n_cmp, n_slc, n_sel):
    n = kc_ref.shape[1]
    q = q_ref[0]
    row = lax.broadcasted_iota(I32, (H_B, 1), 0)
    first = row < G
    kc = kc_ref[0].astype(BF16)
    vc = vc_ref[0].astype(BF16)
    s = jnp.where(first, _dot_nt(q, kc[:, :DH]), _dot_nt(q, kc[:, DH:])) * SCALE
    col = lax.broadcasted_iota(I32, (1, n), 1)
    qpos = PAST_LEN
    pr = _mprobs(s, (col * S_CMP + (L_CMP - 1) <= qpos) & (col < n_cmp))
    pb = pr.astype(BF16)
    o_ref[0] = jnp.where(first, jnp.dot(pb, vc[:, :DH], preferred_element_type=F32), jnp.dot(pb, vc[:, DH:], preferred_element_type=F32))
    psum = jnp.concatenate([jnp.sum(jnp.where((row >= kv * G) & (row < (kv + 1) * G), pr, 0.0), axis=0, keepdims=True)
                            for kv in range(KVH_B)] + [jnp.zeros((8 - KVH_B, n), F32)], axis=0)
    ph, plo = _split(psum)
    imp = jnp.dot(ph, ov_ref[...], preferred_element_type=F32) + jnp.dot(plo, ov_ref[...], preferred_element_type=F32)
    j = lax.broadcasted_iota(I32, imp.shape, 1)
    jf = j.astype(F32)
    jt = qpos // L_SLC
    forced = (j == 0) | (j == jt) | (j == jt - 1)
    sc = jnp.where((j > jt) | (j >= n_slc), NEG, jnp.where(forced, FORCED, imp))
    lane = lax.broadcasted_iota(I32, (8, 128), 1)
    ids = jnp.zeros((8, 128), F32)
    for r in range(n_sel):
        m = jnp.max(sc, axis=1, keepdims=True)
        pick = jnp.min(jnp.where(sc == m, jf, 1e9), axis=1, keepdims=True)
        ids = jnp.where(lane == r, pick, ids)
        sc = jnp.where(jf == pick, -jnp.inf, sc)
    ids_ref[0] = ids.astype(I32)


def _slc_sample_kernel(ids_ref, pt_ref, q_ref, knew_ref, vnew_ref, kblk_ref, vblk_ref, o_ref, kbuf, vbuf, *, n_sel, n_past):
    b, kv, r = pl.program_id(0), pl.program_id(1), pl.program_id(2)
    base = (b * KVH_B + kv) * n_sel
    bid = ids_ref[base + r]
    row = lax.broadcasted_iota(I32, (L_SLC, 1), 0)
    past = bid < n_past
    kbuf[pl.ds(pl.multiple_of(r * L_SLC, L_SLC), L_SLC), :] = jnp.where(
        past, kblk_ref[0, 0, 0], jnp.where(row == 0, knew_ref[0, 0], 0.0))
    vbuf[pl.ds(pl.multiple_of(r * L_SLC, L_SLC), L_SLC), :] = jnp.where(
        past, vblk_ref[0, 0, 0], jnp.where(row == 0, vnew_ref[0, 0], 0.0))

    @pl.when(r == n_sel - 1)
    def _():
        lane = lax.broadcasted_iota(I32, (1, n_sel * L_SLC), 1)
        tpos = lane % L_SLC
        for rr in range(n_sel):
            tpos = tpos + jnp.where(lane // L_SLC == rr, ids_ref[base + rr] * L_SLC, 0)
        s = _dot_nt(q_ref[0, 0], kbuf[...]) * SCALE
        o_ref[0, 0] = _dot(_mprobs(s, tpos <= PAST_LEN), vbuf[...])


def _win_sample_kernel(q_ref, kb_ref, vb_ref, kn_ref, vn_ref, o_ref, ko_ref, vo_ref):
    W = kb_ref.shape[2]
    q = q_ref[0]
    rowq = lax.broadcasted_iota(I32, (H_B, 1), 0)
    first = rowq < G
    kb, vb = kb_ref[0, 0], vb_ref[0, 0]
    kn, vn = kn_ref[0], vn_ref[0]
    s = jnp.where(first, _dot_nt(q, kb[:, :DH]), _dot_nt(q, kb[:, DH:])) * SCALE
    col = lax.broadcasted_iota(I32, (1, W), 1)
    mask = (PAST_LEN - W + col) > (PAST_LEN - WINDOW)
    knh = jnp.where(first, kn[:, :DH], kn[:, DH:])
    vnh = jnp.where(first, vn[:, :DH], vn[:, DH:])
    sn = jnp.sum(q * knh, axis=1, keepdims=True) * SCALE
    s = jnp.where(mask, s, NEG)
    m = jnp.maximum(jnp.max(s, axis=1, keepdims=True), sn)
    p = jnp.where(mask, jnp.exp(s - m), 0.0)
    pn = jnp.exp(sn - m)
    inv = 1.0 / jnp.maximum(jnp.sum(p, axis=1, keepdims=True) + pn, 1e-30)
    pb = (p * inv).astype(BF16)
    vbb = vb.astype(BF16)
    pv = jnp.where(first, jnp.dot(pb, vbb[:, :DH], preferred_element_type=F32), jnp.dot(pb, vbb[:, DH:], preferred_element_type=F32))
    o_ref[0] = pv + (pn * inv) * vnh
    roww = lax.broadcasted_iota(I32, (W, 1), 0)
    ko_ref[0] = jnp.where(roww == W - 1, kn[:1, :], pltpu.roll(kb, W - 1, 0))
    vo_ref[0] = jnp.where(roww == W - 1, vn[:1, :], pltpu.roll(vb, W - 1, 0))


def _nsa_sample(l, z128, zpl, cache_ck, cache_cv, cache_sk, cache_sv, buf_k, buf_v, cw, pt_flat, DB):
    kc, vc = _compress_sample(l, cache_ck, cache_cv, cw, pt_flat, DB)
    n = kc.shape[1]
    L = PAST_LEN + 1
    n_cmp = L // S_CMP - L_CMP // S_CMP + 1
    n_slc = -(-L // L_SLC)
    n_sel = min(N_SEL_MAX, n_slc)
    n_past = PAST_LEN // L_SLC
    cols = -(-n_slc // 128) * 128
    qb = z128[:DB, 1024:2048].reshape(DB, H_B, DH)
    per_b = lambda shape: pl.BlockSpec(shape, lambda b: (b,) + (0,) * (len(shape) - 1))
    o_cmp, ids = pl.pallas_call(
        functools.partial(_cmp_select_sample_kernel, n_cmp=n_cmp, n_slc=n_slc, n_sel=n_sel), grid=(DB,), name='cmp_select_sample',
        in_specs=[per_b((1, H_B, DH)), per_b((1, n, 256)), per_b((1, n, 256)), pl.BlockSpec((n, cols), lambda b: (0, 0))],
        out_specs=[per_b((1, H_B, DH)), per_b((1, 8, 128))],
        out_shape=[jax.ShapeDtypeStruct((DB, H_B, DH), F32), jax.ShapeDtypeStruct((DB, 8, 128), I32)],
        compiler_params=_params('parallel'),
    )(qb, kc, vc, _overlap_np(n_cmp, n_slc, n, cols))
    ids_flat = ids[:, :KVH_B, :n_sel].reshape(-1)
    q4 = jnp.pad(qb.reshape(DB, KVH_B, G, DH), ((0, 0), (0, 0), (0, 8 - G), (0, 0)))
    knew = z128[:DB, 2560:2816].reshape(DB, KVH_B, 1, DH)
    vnew = zpl[:DB, 512:768].reshape(DB, KVH_B, 1, DH)
    bpp = PAGE_SIZE // L_SLC
    n_pages = PAST_LEN // PAGE_SIZE

    def blk_map(b, kv, r, ids_s, pt):
        bid = jnp.minimum(ids_s[(b * KVH_B + kv) * n_sel + r], n_past - 1)
        return (l, pt[b * n_pages + bid // bpp], bid % bpp, 0, kv)

    per_bk = lambda shape: pl.BlockSpec(shape, lambda b, kv, r, ids_s, pt: (b, kv, 0, 0))
    view = lambda c: c.reshape(c.shape[0], c.shape[1], bpp, L_SLC, KVH_B * DH)
    o_slc = pl.pallas_call(
        functools.partial(_slc_sample_kernel, n_sel=n_sel, n_past=n_past), name='slc_sample',
        grid_spec=pltpu.PrefetchScalarGridSpec(
            num_scalar_prefetch=2, grid=(DB, KVH_B, n_sel),
            in_specs=[per_bk((1, 1, 8, DH)), per_bk((1, 1, 1, DH)), per_bk((1, 1, 1, DH)),
                      pl.BlockSpec((1, 1, 1, L_SLC, DH), blk_map), pl.BlockSpec((1, 1, 1, L_SLC, DH), blk_map)],
            out_specs=per_bk((1, 1, 8, DH)),
            scratch_shapes=[pltpu.VMEM((n_sel * L_SLC, DH), F32)] * 2),
        out_shape=jax.ShapeDtypeStruct((DB, KVH_B, 8, DH), F32),
        compiler_params=_params('parallel', 'parallel', 'arbitrary'),
    )(ids_flat, pt_flat, q4, knew, vnew, view(cache_sk), view(cache_sv))
    W = buf_k.shape[2]
    kn8 = jnp.broadcast_to(z128[:DB, None, 2816:3072], (DB, 8, 256))
    vn8 = jnp.broadcast_to(zpl[:DB, None, 768:1024], (DB, 8, 256))
    bufspec = pl.BlockSpec((1, 1, W, 256), lambda b: (l, b, 0, 0))
    o_win, nk, nv = pl.pallas_call(
        _win_sample_kernel, grid=(DB,), name='win_sample',
        in_specs=[per_b((1, H_B, DH)), bufspec, bufspec, per_b((1, 8, 256)), per_b((1, 8, 256))],
        out_specs=[per_b((1, H_B, DH)), per_b((1, W, 256)), per_b((1, W, 256))],
        out_shape=[jax.ShapeDtypeStruct((DB, H_B, DH), F32)] + [jax.ShapeDtypeStruct((DB, W, 256), F32)] * 2,
        compiler_params=_params('parallel'),
    )(qb, buf_k.reshape(buf_k.shape[0], DB, W, 256), buf_v.reshape(buf_v.shape[0], DB, W, 256), kn8, vn8)
    gb = jax.nn.sigmoid(zpl[:DB, 1024 + GATE_OFF:1024 + GATE_OFF + 3 * H_B]).reshape(DB, H_B, 3)
    o_b = gb[..., 0:1] * o_cmp + gb[..., 1:2] * o_slc[:, :, :G].reshape(DB, H_B, DH) + gb[..., 2:3] * o_win
    return o_b.reshape(DB, H_B * DH), nk, nv


def kernel(x_prompt, x_sample, cache_dsa_k, cache_dsa_v, cache_idx_k, cache_cmp_k, cache_cmp_v, cache_slc_k, cache_slc_v, state_win_k, state_win_v, page_table, norm_mix, w_in, cmp_pe_k, cmp_w1_k, cmp_w2_k, cmp_pe_v, cmp_w1_v, cmp_w2_v, w_up_a, w_up_b, w_o, norm_ffn, dense_w_gate, dense_w_up, dense_w_down, moe_w_router, moe_w_gate, moe_w_up, moe_w_down, norm_final):
    B, S, D = x_prompt.shape
    DB, T, _ = x_sample.shape
    depth = w_in.shape[0]
    assert T == 1 and D == D_MODEL and page_table.shape == (DB, PAST_LEN // PAGE_SIZE)
    wp = min(WINDOW, S)
    xp = x_prompt.reshape(B * S, D)
    xs = jnp.pad(x_sample.reshape(DB, D), ((0, SAMPLE_ROWS - DB), (0, 0)))
    pt_flat = page_table.reshape(-1).astype(I32)
    pos_p = jnp.tile(jnp.arange(S), B)
    pos_s = jnp.full((SAMPLE_ROWS,), PAST_LEN)
    tabs_p = (_rope_tables(pos_p, ROT_DIM, DH), _rope_tables(pos_p, ROT_IDX, D_IDX))
    tabs_s = (_rope_tables(pos_s, ROT_DIM, DH), _rope_tables(pos_s, ROT_IDX, D_IDX))
    pool2 = lambda c: c.reshape(c.shape[0], c.shape[1], PAGE_SIZE, -1)
    c_dsa_k, c_dsa_v, c_idx, c_cmp_k, c_cmp_v, c_slc_k, c_slc_v = map(
        pool2, (cache_dsa_k, cache_dsa_v, cache_idx_k, cache_cmp_k, cache_cmp_v, cache_slc_k, cache_slc_v))
    newp, news = [], []
    for l in range(depth):
        cw = (cmp_pe_k[l], cmp_w1_k[l], cmp_w2_k[l], cmp_pe_v[l], cmp_w1_v[l], cmp_w2_v[l])
        wsplit = _split_w_in(w_in[l])
        wa, wb, wo = w_up_a[l].astype(BF16), w_up_b[l].astype(BF16), w_o[l].astype(BF16)

        z128, z64, zpl, zg = _project(_rms(xp, norm_mix[l], BF16), wsplit, *tabs_p)
        z128b, z64b, zplb = (z.reshape(B, S, -1) for z in (z128, z64, zpl))
        o_a = _dsa_prompt(z128b, z64b, zplb, B, S)
        kc, vc = _compress_prompt(z128b, zplb, cw, B, S)
        o_b = _nsa_prompt(z128b, zplb, kc, vc, B, S)
        xp = _merge(xp, o_a.reshape(B * S, -1), o_b.reshape(B * S, -1), zg, wa, wb, wo)
        kv4 = lambda z, c: z[:, :, c:c + 256].reshape(B, S, 2, DH)
        newp.append((kv4(z128b, 2048), kv4(zplb, 0), z64b[:, :, 1024:1024 + D_IDX], kv4(z128b, 2304), kv4(zplb, 256),
                     kv4(z128b, 2560), kv4(zplb, 512), kv4(z128b, 2816)[:, S - wp:], kv4(zplb, 768)[:, S - wp:]))

        y128, y64, ypl, yg = _project(_rms(xs, norm_mix[l], BF16), wsplit, *tabs_s)
        o_a = _dsa_sample(l, y128, y64, ypl, c_dsa_k, c_dsa_v, c_idx, pt_flat, DB)
        o_b, win_k, win_v = _nsa_sample(l, y128, ypl, c_cmp_k, c_cmp_v, c_slc_k, c_slc_v, state_win_k, state_win_v, cw, pt_flat, DB)
        pad = lambda o: jnp.pad(o, ((0, SAMPLE_ROWS - DB), (0, 0)))
        xs = _merge(xs, pad(o_a), pad(o_b), yg, wa, wb, wo)
        s4 = lambda z, c: z[:DB, c:c + 256].reshape(DB, 1, 2, DH)
        news.append((s4(y128, 2048), s4(ypl, 0), y64[:DB, 1024:1024 + D_IDX].reshape(DB, 1, D_IDX), s4(y128, 2304), s4(ypl, 256),
                     s4(y128, 2560), s4(ypl, 512), win_k.reshape(DB, -1, 2, DH), win_v.reshape(DB, -1, 2, DH)))

        i = l // 2
        if l % 2 == 0:
            ws = (dense_w_gate[i].astype(BF16), dense_w_up[i].astype(BF16), dense_w_down[i].astype(BF16))
            xp = _dense_ffn(xp, norm_ffn[l], *ws)
            xs = _dense_ffn(xs, norm_ffn[l], *ws)
        else:
            xp, xs = _moe(xp, xs, norm_ffn[l], moe_w_router[i], moe_w_gate[i].astype(BF16), moe_w_up[i].astype(BF16),
                          moe_w_down[i].astype(BF16))

    y_prompt = _rms(xp, norm_final, F32).reshape(B, S, D)
    y_sample = _rms(xs, norm_final, F32)[:DB].reshape(DB, 1, D)
    stack = lambda items: [jnp.stack([it[n] for it in items], axis=0) for n in range(9)]
    return (y_prompt, y_sample, *stack(newp), *stack(news))
```

```python
import functools

import numpy as np
import jax
import jax.numpy as jnp
from jax import lax
from jax.experimental import pallas as pl
from jax.experimental.pallas import tpu as pltpu

F32 = jnp.float32
BF16 = jnp.bfloat16
I32 = jnp.int32

D_MODEL = 2048
PAST_LEN = 16384
PAGE_SIZE = 128
DH = 128
ROT_DIM = DH // 4
ROPE_THETA = 500000.0
H_A = 8
KVH_A = 2
H_I = 16
D_IDX = 64
ROT_IDX = D_IDX // 4
TOPK_MAX = 256
H_B = 8
KVH_B = 2
G = 4
L_CMP = 32
S_CMP = 16
L_SLC = 64
N_SEL_MAX = 16
WINDOW = 512
N_EXP = 8
Q_BLK = 128
KEY_CLS = 512
EPS = 1e-6
NEG = -1e30
FORCED = 1e6
INT_MIN = -2 ** 31
SCALE = DH ** -0.5
PAGES_PER_STEP = 16
SAMPLE_ROWS = 16
VMEM_LIMIT = 48 * 1024 * 1024

_IN_SPLITS = (
    ('q_a', H_A * DH), ('k_a', KVH_A * DH), ('v_a', KVH_A * DH),
    ('q_i', H_I * D_IDX), ('k_i', D_IDX), ('w_i', H_I),
    ('q_b', H_B * DH),
    ('k_cmp', KVH_B * DH), ('v_cmp', KVH_B * DH),
    ('k_slc', KVH_B * DH), ('v_slc', KVH_B * DH),
    ('k_win', KVH_B * DH), ('v_win', KVH_B * DH),
    ('g_b', 3 * H_B), ('g_mix', 2 * D_MODEL),
)
_GROUP_ROPE128 = ('q_a', 'q_b', 'k_a', 'k_cmp', 'k_slc', 'k_win')
_GROUP_ROPE64 = ('q_i', 'k_i')
_GROUP_PLAIN = ('v_a', 'v_cmp', 'v_slc', 'v_win', 'w_i', 'g_b')
W_R64 = 1280
W_PLAIN = 1280
GATE_OFF = H_I


def _params(*sem):
    return pltpu.CompilerParams(dimension_semantics=sem, vmem_limit_bytes=VMEM_LIMIT)


def _dot(a, b):
    return jnp.dot(a.astype(BF16), b.astype(BF16), preferred_element_type=F32)


def _dg_nt(a, b):
    return lax.dot_general(a, b, (((1,), (1,)), ((), ())), preferred_element_type=F32)


def _dot_nt(a, b):
    return _dg_nt(a.astype(BF16), b.astype(BF16))


def _split(a):
    hi = a.astype(BF16)
    lo = (a - hi.astype(F32)).astype(BF16)
    return hi, lo


def _dot3_nt(a, b):
    ah, al = _split(a)
    bh, bl = _split(b)
    return _dg_nt(ah, bh) + _dg_nt(ah, bl) + _dg_nt(al, bh)


def _mprobs(s, mask):
    s = jnp.where(mask, s, NEG)
    m = jnp.max(s, axis=-1, keepdims=True)
    p = jnp.where(mask, jnp.exp(s - m), 0.0)
    return p * (1.0 / jnp.maximum(jnp.sum(p, axis=-1, keepdims=True), 1e-30))


def _order_key(x):
    b = lax.bitcast_convert_type(x + 0.0, I32)
    return jnp.where(b < 0, b ^ jnp.int32(0x7FFFFFFF), b)


def _kth_largest_key(count_ge, rows, k, n_bits=32):
    def body(it, lo):
        cand = lo + jnp.left_shift(jnp.int32(1), 31 - it)
        return jnp.where(count_ge(cand) >= k, cand, lo)
    return lax.fori_loop(0, n_bits, body, jnp.full((rows, 1), INT_MIN, I32))


def _rms_kernel(x_ref, g_ref, o_ref):
    x = x_ref[...]
    y = x * lax.rsqrt(jnp.mean(x * x, axis=-1, keepdims=True) + EPS)
    o_ref[...] = (y * g_ref[...]).astype(o_ref.dtype)


def _rms(x, g, out_dtype):
    M, D = x.shape
    tm = min(M, 512)
    return pl.pallas_call(
        _rms_kernel, grid=(M // tm,), name='rms',
        in_specs=[pl.BlockSpec((tm, D), lambda i: (i, 0)), pl.BlockSpec((1, D), lambda i: (0, 0))],
        out_specs=pl.BlockSpec((tm, D), lambda i: (i, 0)),
        out_shape=jax.ShapeDtypeStruct((M, D), out_dtype),
        compiler_params=_params('parallel'),
    )(x, g.reshape(1, D))


def _mm_kernel(*refs, mode, shift):
    a_ref, b_ref = refs[0], refs[1]
    o_ref = refs[-1]
    acc = jnp.dot(a_ref[...], b_ref[...], preferred_element_type=F32)
    if mode == 'rope':
        c_ref, s1_ref, s2_ref = refs[2:5]
        tn = acc.shape[1]
        rep = tn // 128
        til = lambda r: jnp.concatenate([r[...]] * rep, axis=1)
        acc = (acc * til(c_ref) + pltpu.roll(acc, tn - shift, 1) * til(s1_ref)
               + pltpu.roll(acc, shift, 1) * til(s2_ref))
    elif mode == 'sigmoid':
        acc = jax.nn.sigmoid(acc)
    elif mode == 'res':
        acc = acc + refs[2][...]
    o_ref[...] = acc.astype(o_ref.dtype)


def _mm(a, b, tn, mode='none', shift=0, extra=(), out_dtype=F32, name='mm'):
    M, K = a.shape
    N = b.shape[1]
    tm = min(M, 1024)
    in_specs = [pl.BlockSpec((tm, K), lambda i, j: (i, 0)), pl.BlockSpec((K, tn), lambda i, j: (0, j))]
    if mode == 'rope':
        in_specs += [pl.BlockSpec((tm, 128), lambda i, j: (i, 0))] * 3
    elif mode == 'res':
        in_specs += [pl.BlockSpec((tm, tn), lambda i, j: (i, j))]
    return pl.pallas_call(
        functools.partial(_mm_kernel, mode=mode, shift=shift), grid=(M // tm, N // tn), name=name,
        in_specs=in_specs, out_specs=pl.BlockSpec((tm, tn), lambda i, j: (i, j)),
        out_shape=jax.ShapeDtypeStruct((M, N), out_dtype),
        compiler_params=_params('parallel', 'parallel'),
    )(a, b, *extra)


def _rope_tables(pos, rot, period):
    half = rot // 2
    inv = jnp.power(jnp.float32(ROPE_THETA), -jnp.arange(half, dtype=F32) / half)
    ang = pos.astype(F32)[:, None] * inv[None, :]
    cos, sin = jnp.cos(ang), jnp.sin(ang)
    T = pos.shape[0]
    z = jnp.zeros((T, period - rot), F32)
    zh = jnp.zeros((T, half), F32)
    c = jnp.concatenate([cos, cos, jnp.ones((T, period - rot), F32)], axis=1)
    s1 = jnp.concatenate([-sin, zh, z], axis=1)
    s2 = jnp.concatenate([zh, sin, z], axis=1)
    rep = 128 // period
    return tuple(jnp.tile(t, (1, rep)) for t in (c, s1, s2))


def _split_w_in(w):
    parts, off = {}, 0
    for name, width in _IN_SPLITS:
        parts[name] = w[:, off:off + width]
        off += width
    cat = lambda names, total: jnp.pad(
        jnp.concatenate([parts[n] for n in names], axis=1),
        ((0, 0), (0, total - sum(parts[n].shape[1] for n in names)))).astype(BF16)
    return (cat(_GROUP_ROPE128, 3072), cat(_GROUP_ROPE64, W_R64), cat(_GROUP_PLAIN, W_PLAIN),
            parts['g_mix'].astype(BF16))


def _project(h, wsplit, tabs128, tabs64):
    w128, w64, wpl, wg = wsplit
    z128 = _mm(h, w128, 512, 'rope', ROT_DIM // 2, tabs128, name='proj_rope128')
    z64 = _mm(h, w64, 640, 'rope', ROT_IDX // 2, tabs64, name='proj_rope64')
    zpl = _mm(h, wpl, 640, name='proj_plain')
    zg = _mm(h, wg, 512, 'sigmoid', name='proj_gmix')
    return z128, z64, zpl, zg


def _stack_heads(q_ref, kv):
    return jnp.concatenate([q_ref[0, :, (kv * G + g) * DH:(kv * G + g + 1) * DH] for g in range(G)], axis=0).astype(BF16)


def _masked_attend(q, k, v, mask):
    s = _dg_nt(q, k) * SCALE
    p = jnp.concatenate([_mprobs(s[g * Q_BLK:(g + 1) * Q_BLK], mask).astype(BF16) for g in range(G)], axis=0)
    return jnp.dot(p, v, preferred_element_type=F32)


def _dsa_prompt_body(qa_ref, ka_ref, va_ref, qi_ref, ki_ref, wi_ref, o_ref, *, k_sel, ext):
    i = pl.program_id(1)
    w = wi_ref[0][:, :H_I] * (H_I ** -0.5)
    npair = H_I * D_IDX // 128
    qh, ql = _split(jnp.concatenate([qi_ref[0, :, j * 128:(j + 1) * 128] for j in range(npair)], axis=0))
    parts = []
    for c0 in range(0, ext, KEY_CLS):
        kb = ki_ref[0, c0:c0 + KEY_CLS, :]
        kh, kl = _split(jnp.concatenate([kb, pltpu.roll(kb, D_IDX, 1)], axis=0))
        d = (_dg_nt(qh, kh) + _dg_nt(qh, kl) + _dg_nt(ql, kh)) * (D_IDX ** -0.5)
        sc = None
        for j in range(npair):
            dj = d[j * Q_BLK:(j + 1) * Q_BLK]
            t = (jnp.maximum(dj[:, :KEY_CLS], 0.0) * w[:, 2 * j:2 * j + 1]
                 + jnp.maximum(dj[:, KEY_CLS:], 0.0) * w[:, 2 * j + 1:2 * j + 2])
            sc = t if sc is None else sc + t
        parts.append(sc)
    score = jnp.concatenate(parts, axis=1)
    qpos = i * Q_BLK + lax.broadcasted_iota(I32, (Q_BLK, 1), 0)
    kpos = lax.broadcasted_iota(I32, (1, ext), 1)
    masked = jnp.where(kpos <= qpos, score, NEG)
    key = jnp.where(masked > NEG * 0.5, _order_key(masked), INT_MIN)
    count_ge = lambda c: jnp.sum(jnp.where(key >= c, 1.0, 0.0), axis=1, keepdims=True)
    n_it = jnp.where((i + 1) * Q_BLK > k_sel, 32, 0)
    thr = _kth_largest_key(count_ge, Q_BLK, k_sel, n_it)
    sel = key >= jnp.maximum(thr, INT_MIN + 1)
    for kv in range(KVH_A):
        sl = slice(kv * DH, (kv + 1) * DH)
        o = _masked_attend(_stack_heads(qa_ref, kv), ka_ref[0, :ext, sl].astype(BF16), va_ref[0, :ext, sl].astype(BF16), sel)
        for g in range(G):
            o_ref[0, :, (kv * G + g) * DH:(kv * G + g + 1) * DH] = o[g * Q_BLK:(g + 1) * Q_BLK]


def _by_key_class(body, S, **kw):
    def kern(*refs):
        i = pl.program_id(1)
        for c in range(S // KEY_CLS):
            pl.when(i // (KEY_CLS // Q_BLK) == c)(functools.partial(body, *refs, ext=(c + 1) * KEY_CLS, **kw))
    return kern


def _dsa_prompt(z128, z64, zpl, B, S):
    k_sel = min(TOPK_MAX, S // 4)
    blk = lambda w, f: pl.BlockSpec((1, Q_BLK, w), f)
    full = lambda w, c: pl.BlockSpec((1, S, w), lambda b, i: (b, 0, c))
    return pl.pallas_call(
        _by_key_class(_dsa_prompt_body, S, k_sel=k_sel), grid=(B, S // Q_BLK), name='dsa_prompt',
        in_specs=[blk(1024, lambda b, i: (b, i, 0)), full(256, 8), full(256, 0),
                  blk(1024, lambda b, i: (b, i, 0)), full(128, 8), blk(128, lambda b, i: (b, i, 8))],
        out_specs=blk(1024, lambda b, i: (b, i, 0)),
        out_shape=jax.ShapeDtypeStruct((B, S, H_A * DH), F32),
        compiler_params=_params('parallel', 'parallel'),
    )(z128, z128, zpl, z64, z64, zpl)


def _cmp_partial_kernel(*refs, n_src, paged):
    refs = refs[len(refs) - (2 * n_src + 8):]
    rw = KVH_B * DH

    def rows_of(r, p, kvh):
        if paged:
            return r[0, 0, pl.ds(KVH_B * p + kvh, PAGE_SIZE // S_CMP, stride=KVH_B * S_CMP), :]
        return r[0, :, p * rw + kvh * DH:p * rw + (kvh + 1) * DH]
    ksrc, vsrc = refs[:n_src], refs[n_src:2 * n_src]
    pek, w1k, pev, w1v, ak, bk, av, bv = refs[2 * n_src:]
    half = L_CMP // 2
    for srcs, pe_ref, w1_ref, a_ref, b_ref in ((ksrc, pek, w1k, ak, bk), (vsrc, pev, w1v, av, bv)):
        for kvh in range(KVH_B):
            acc_a = acc_b = None
            for p in range(half):
                rows = jnp.concatenate([rows_of(r, p, kvh) for r in srcs], axis=0)
                da = _dot(rows + pe_ref[p:p + 1, :], w1_ref[p * DH:(p + 1) * DH, :])
                db = _dot(rows + pe_ref[half + p:half + p + 1, :], w1_ref[(half + p) * DH:(half + p + 1) * DH, :])
                acc_a = da if acc_a is None else acc_a + da
                acc_b = db if acc_b is None else acc_b + db
            a_ref[0, :, kvh * DH:(kvh + 1) * DH] = acc_a
            b_ref[0, :, kvh * DH:(kvh + 1) * DH] = acc_b


def _cmp_finish_kernel(ak_ref, bk_ref, av_ref, bv_ref, w2k_ref, w2v_ref, kc_ref, vc_ref):
    n = ak_ref.shape[1]
    for a_ref, b_ref, w2_ref, o_ref in ((ak_ref, bk_ref, w2k_ref, kc_ref), (av_ref, bv_ref, w2v_ref, vc_ref)):
        for kvh in range(KVH_B):
            sl = slice(kvh * DH, (kvh + 1) * DH)
            h = a_ref[0, :, sl] + pltpu.roll(b_ref[0, :, sl], n - 1, 0)
            o_ref[0, :, sl] = _dot(jax.nn.gelu(h), w2_ref[...])


def _cmp_finish(parts, w2k, w2v):
    Bt, n, _ = parts[0].shape
    spec = pl.BlockSpec((1, n, KVH_B * DH), lambda b: (b, 0, 0))
    wspec = pl.BlockSpec((DH, DH), lambda b: (0, 0))
    return pl.pallas_call(
        _cmp_finish_kernel, grid=(Bt,), name='cmp_finish',
        in_specs=[spec] * 4 + [wspec] * 2, out_specs=[spec] * 2,
        out_shape=[jax.ShapeDtypeStruct((Bt, n, KVH_B * DH), F32)] * 2,
        compiler_params=_params('parallel'),
    )(*parts, w2k.astype(BF16), w2v.astype(BF16))


def _cmp_weight_specs(imap):
    return [pl.BlockSpec((L_CMP, DH), imap), pl.BlockSpec((L_CMP * DH, DH), imap)] * 2


def _compress_prompt(z128, zpl, cw, B, S):
    pek, w1k, w2k, pev, w1v, w2v = cw
    nch = S // S_CMP
    const = lambda b: (0, 0)
    out_spec = pl.BlockSpec((1, nch, KVH_B * DH), lambda b: (b, 0, 0))
    src_spec = pl.BlockSpec((1, nch, S_CMP * 256), lambda b: (b, 0, 0))
    chunked = lambda z, c: z[:, :, c:c + 256].reshape(B, nch, S_CMP * 256)
    parts = pl.pallas_call(
        functools.partial(_cmp_partial_kernel, n_src=1, paged=False), grid=(B,), name='cmp_partial_prompt',
        in_specs=[src_spec, src_spec] + _cmp_weight_specs(const),
        out_specs=[out_spec] * 4,
        out_shape=[jax.ShapeDtypeStruct((B, nch, KVH_B * DH), F32)] * 4,
        compiler_params=_params('parallel'),
    )(chunked(z128, 2304), chunked(zpl, 256), pek, w1k.astype(BF16), pev, w1v.astype(BF16))
    return _cmp_finish(parts, w2k, w2v)


def _nsa_prompt_body(qb_ref, kc_ref, vc_ref, ks_ref, vs_ref, kw_ref, vw_ref, gt_ref, ov_ref, ex_ref, o_ref,
                     *, n_cmp, n_slc, n_sel, ext):
    i = pl.program_id(1)
    S = ks_ref.shape[1]
    wspan = WINDOW + Q_BLK
    qpos = i * Q_BLK + lax.broadcasted_iota(I32, (Q_BLK, 1), 0)
    lane = lax.broadcasted_iota(I32, (1, 128), 1)
    kpos = lax.broadcasted_iota(I32, (1, ext), 1)
    causal = kpos <= qpos
    vis = (lane * S_CMP + (L_CMP - 1) <= qpos) & (lane < n_cmp)
    jt = qpos // L_SLC
    forced = (lane == 0) | (lane == jt) | (lane == jt - 1)
    wstart = pl.multiple_of(jnp.clip(i * Q_BLK - WINDOW, 0, S - wspan), Q_BLK)
    wpos = wstart + lax.broadcasted_iota(I32, (1, wspan), 1)
    wmask = (wpos <= qpos) & (wpos > qpos - WINDOW)
    gates = jax.nn.sigmoid(gt_ref[0])
    for kv in range(KVH_B):
        sl = slice(kv * DH, (kv + 1) * DH)
        q = _stack_heads(qb_ref, kv)
        s_cmp = _dg_nt(q, kc_ref[0, :, sl].astype(BF16)) * SCALE
        prs = [_mprobs(s_cmp[g * Q_BLK:(g + 1) * Q_BLK], vis) for g in range(G)]
        o_cmp = _dot(jnp.concatenate(prs, axis=0), vc_ref[0, :, sl])
        ph, plo = _split(prs[0] + prs[1] + prs[2] + prs[3])
        imp = jnp.dot(ph, ov_ref[...], preferred_element_type=F32) + jnp.dot(plo, ov_ref[...], preferred_element_type=F32)
        sc = jnp.where(lane > jt, NEG, jnp.where(forced, FORCED, imp))
        rank = jnp.zeros((Q_BLK, 128), F32)
        for j2 in range(n_slc):
            col = sc[:, j2:j2 + 1]
            rank = rank + jnp.where(col > sc, 1.0, jnp.where((col == sc) & (lane > j2), 1.0, 0.0))
        sel = jnp.where((rank < n_sel) & (sc > NEG * 0.5), 1.0, 0.0).astype(BF16)
        tokmask = jnp.where(causal, jnp.dot(sel, ex_ref[:, :ext], preferred_element_type=F32), 0.0) > 0.5
        o_slc = _masked_attend(q, ks_ref[0, :ext, sl].astype(BF16), vs_ref[0, :ext, sl].astype(BF16), tokmask)
        o_win = _masked_attend(q, kw_ref[0, pl.ds(wstart, wspan), sl].astype(BF16),
                               vw_ref[0, pl.ds(wstart, wspan), sl].astype(BF16), wmask)
        for g in range(G):
            hd = kv * G + g
            rows = slice(g * Q_BLK, (g + 1) * Q_BLK)
            c0 = GATE_OFF + hd * 3
            o_ref[0, :, hd * DH:(hd + 1) * DH] = (gates[:, c0:c0 + 1] * o_cmp[rows] + gates[:, c0 + 1:c0 + 2] * o_slc[rows]
                                                  + gates[:, c0 + 2:c0 + 3] * o_win[rows])


def _overlap_np(nc, nslc, rows, cols):
    cs = np.arange(rows)[:, None] * S_CMP
    js = np.arange(cols)[None, :] * L_SLC
    ov = (cs < js + L_SLC) & (cs + L_CMP > js) & (np.arange(rows)[:, None] < nc) & (np.arange(cols)[None, :] < nslc)
    return jnp.asarray(ov.astype(np.float32), dtype=BF16)


def _nsa_prompt(z128, zpl, kc, vc, B, S):
    n_cmp = S // S_CMP - L_CMP // S_CMP + 1
    n_slc = -(-S // L_SLC)
    assert n_cmp <= 128 and n_slc <= 128 and kc.shape[1] == 128
    ov = _overlap_np(n_cmp, n_slc, 128, 128)
    ex = jnp.asarray((np.arange(128)[:, None] == np.arange(S)[None, :] // L_SLC).astype(np.float32), dtype=BF16)
    blk = lambda w, f: pl.BlockSpec((1, Q_BLK, w), f)
    full = lambda w, c: pl.BlockSpec((1, S, w), lambda b, i: (b, 0, c))
    const = lambda shape: pl.BlockSpec(shape, lambda b, i: (0, 0))
    cspec = pl.BlockSpec((1, 128, 256), lambda b, i: (b, 0, 0))
    return pl.pallas_call(
        _by_key_class(_nsa_prompt_body, S, n_cmp=n_cmp, n_slc=n_slc, n_sel=min(N_SEL_MAX, n_slc)),
        grid=(B, S // Q_BLK), name='nsa_prompt',
        in_specs=[blk(1024, lambda b, i: (b, i, 1)), cspec, cspec, full(256, 10), full(256, 2), full(256, 11), full(256, 3),
                  blk(128, lambda b, i: (b, i, 8)), const((128, 128)), const((128, S))],
        out_specs=blk(1024, lambda b, i: (b, i, 0)),
        out_shape=jax.ShapeDtypeStruct((B, S, H_B * DH), F32),
        compiler_params=_params('parallel', 'parallel'),
    )(z128, kc, vc, z128, zpl, z128, zpl, zpl, ov, ex)


def _merge_u_kernel(oa_ref, ob_ref, wa_ref, wb_ref, ga_ref, gb_ref, o_ref):
    o_ref[...] = (ga_ref[...] * _dot(oa_ref[...], wa_ref[...]) + gb_ref[...] * _dot(ob_ref[...], wb_ref[...])).astype(o_ref.dtype)


def _merge(x, o_a, o_b, zg, wa, wb, wo):
    M = x.shape[0]
    tm, tn = min(M, 512), 512
    nj = D_MODEL // tn
    u = pl.pallas_call(
        _merge_u_kernel, grid=(M // tm, nj), name='merge_u',
        in_specs=[pl.BlockSpec((tm, H_A * DH), lambda i, j: (i, 0)), pl.BlockSpec((tm, H_B * DH), lambda i, j: (i, 0)),
                  pl.BlockSpec((H_A * DH, tn), lambda i, j: (0, j)), pl.BlockSpec((H_B * DH, tn), lambda i, j: (0, j)),
                  pl.BlockSpec((tm, tn), lambda i, j: (i, j)), pl.BlockSpec((tm, tn), lambda i, j: (i, j + nj))],
        out_specs=pl.BlockSpec((tm, tn), lambda i, j: (i, j)),
        out_shape=jax.ShapeDtypeStruct((M, D_MODEL), BF16),
        compiler_params=_params('parallel', 'parallel'),
    )(o_a, o_b, wa, wb, zg, zg)
    return _mm(u, wo, 512, 'res', extra=(x,), name='merge_out')


def _ffn_kernel(te_ref, nu_ref, h_ref, wg_ref, wu_ref, wd_ref, *rest, has_res):
    o_ref, acc_ref = rest[-2], rest[-1]
    m, f = pl.program_id(0), pl.program_id(1)

    @pl.when(f == 0)
    def _():
        acc_ref[...] = jnp.zeros_like(acc_ref)

    @pl.when(m < nu_ref[0])
    def _():
        h = h_ref[...]
        a = jnp.dot(h, wg_ref[0], preferred_element_type=F32)
        b = jnp.dot(h, wu_ref[0], preferred_element_type=F32)
        acc_ref[...] += _dot(a * jax.nn.sigmoid(a) * b, wd_ref[0])

    @pl.when(f == pl.num_programs(1) - 1)
    def _():
        o_ref[...] = acc_ref[...] + rest[0][...] if has_res else acc_ref[...]


def _ffn(h, wg, wu, wd, tile_expert, n_used, res=None):
    R, D = h.shape
    F = wg.shape[2]
    tm, tf = min(R, 512), 512
    nf = F // tf
    fe = lambda m, f, nu: jnp.where(m < nu[0], f, nf - 1)
    in_specs = [pl.BlockSpec((tm, D), lambda m, f, te, nu: (m, 0)),
                pl.BlockSpec((1, D, tf), lambda m, f, te, nu: (te[m], 0, fe(m, f, nu))),
                pl.BlockSpec((1, D, tf), lambda m, f, te, nu: (te[m], 0, fe(m, f, nu))),
                pl.BlockSpec((1, tf, D), lambda m, f, te, nu: (te[m], fe(m, f, nu), 0))]
    args = [h, wg, wu, wd]
    if res is not None:
        in_specs.append(pl.BlockSpec((tm, D), lambda m, f, te, nu: (m, 0)))
        args.append(res)
    return pl.pallas_call(
        functools.partial(_ffn_kernel, has_res=res is not None), name='ffn',
        grid_spec=pltpu.PrefetchScalarGridSpec(
            num_scalar_prefetch=2, grid=(R // tm, nf), in_specs=in_specs,
            out_specs=pl.BlockSpec((tm, D), lambda m, f, te, nu: (m, 0)),
            scratch_shapes=[pltpu.VMEM((tm, D), F32)]),
        out_shape=jax.ShapeDtypeStruct((R, D), F32),
        compiler_params=_params('parallel', 'arbitrary'),
    )(tile_expert, n_used, *args)


def _dense_ffn(x, g, wg, wu, wd):
    h = _rms(x, g, BF16)
    n_tiles = x.shape[0] // min(x.shape[0], 512)
    return _ffn(h, wg[None], wu[None], wd[None], jnp.zeros((n_tiles,), I32), jnp.full((1,), n_tiles, I32), res=x)


def _router_kernel(x_ref, g_ref, wh_ref, wl_ref, h_ref, idx_ref, gate_ref):
    x = x_ref[...]
    h = x * lax.rsqrt(jnp.mean(x * x, axis=-1, keepdims=True) + EPS) * g_ref[...]
    h_ref[...] = h.astype(BF16)
    hh, hl = _split(h)
    d = lambda a, b: jnp.dot(a, b, preferred_element_type=F32)
    logits = d(hh, wh_ref[...]) + d(hh, wl_ref[...]) + d(hl, wh_ref[...])
    lane = lax.broadcasted_iota(I32, logits.shape, 1)
    lanef = lane.astype(F32)
    l1 = jnp.where(lane < N_EXP, logits, -jnp.inf)
    m1 = jnp.max(l1, axis=1, keepdims=True)
    i1 = jnp.min(jnp.where(l1 == m1, lanef, 128.0), axis=1, keepdims=True)
    l2 = jnp.where(lanef == i1, -jnp.inf, l1)
    m2 = jnp.max(l2, axis=1, keepdims=True)
    i2 = jnp.min(jnp.where(l2 == m2, lanef, 128.0), axis=1, keepdims=True)
    e = jnp.exp(m2 - m1)
    inv = 1.0 / (1.0 + e)
    idx_ref[...] = jnp.where(lane == 0, i1, i2).astype(I32)
    gate_ref[...] = jnp.where(lane == 0, inv, e * inv)


def _router(x, g, w_r):
    M, D = x.shape
    tm = min(M, 512)
    wpad = jnp.pad(w_r, ((0, 0), (0, 128 - N_EXP)))
    wh = wpad.astype(BF16)
    wl = (wpad - wh.astype(F32)).astype(BF16)
    row = lambda w, dt: (pl.BlockSpec((tm, w), lambda i: (i, 0)), jax.ShapeDtypeStruct((M, w), dt))
    outs = [row(D, BF16), row(128, I32), row(128, F32)]
    h, idx, gate = pl.pallas_call(
        _router_kernel, grid=(M // tm,), name='router',
        in_specs=[pl.BlockSpec((tm, D), lambda i: (i, 0)), pl.BlockSpec((1, D), lambda i: (0, 0)),
                  pl.BlockSpec((D, 128), lambda i: (0, 0)), pl.BlockSpec((D, 128), lambda i: (0, 0))],
        out_specs=[o[0] for o in outs], out_shape=[o[1] for o in outs],
        compiler_params=_params('parallel'),
    )(x, g.reshape(1, D), wh, wl)
    return h, idx[:, :2], gate[:, :2]


def _moe(xp, xs, g, w_r, wg, wu, wd):
    hp, ip, gp = _router(xp, g, w_r)
    hs, is_, gs = _router(xs, g, w_r)
    h = jnp.concatenate([hp, hs], axis=0)
    idx = jnp.concatenate([ip, is_], axis=0)
    gate = jnp.concatenate([gp, gs], axis=0)
    M = h.shape[0]
    tm = 512
    n_tiles = (2 * M + N_EXP * (tm - 1) + tm - 1) // tm
    e = jnp.concatenate([idx[:, 0], idx[:, 1]])
    onehot = (e[:, None] == jnp.arange(N_EXP, dtype=I32)[None, :]).astype(I32)
    rank = jnp.sum((jnp.cumsum(onehot, axis=0) - onehot) * onehot, axis=1)
    counts = jnp.sum(onehot, axis=0)
    tiles_per = (counts + tm - 1) // tm
    tile_end = jnp.cumsum(tiles_per)
    row_start = (tile_end - tiles_per) * tm
    pos = jnp.sum(onehot * row_start[None, :], axis=1) + rank
    src_tok = jnp.zeros((n_tiles * tm,), I32).at[pos].set(jnp.arange(2 * M, dtype=I32) % M)
    n_used = tile_end[-1].astype(I32)
    t = jnp.minimum(jnp.arange(n_tiles, dtype=I32), n_used - 1)
    tile_expert = jnp.sum((t[:, None] >= tile_end[None, :]).astype(I32), axis=1)
    y = _ffn(jnp.take(h, src_tok, axis=0), wg, wu, wd, tile_expert, n_used.reshape(1))
    f = gate[:, 0:1] * jnp.take(y, pos[:M], axis=0) + gate[:, 1:2] * jnp.take(y, pos[M:], axis=0)
    Mp = xp.shape[0]
    return xp + f[:Mp], xs + f[Mp:]


def _page_specs(block, n_lead, layer, col=None):
    def spec(j):
        def imap(b, s, pt):
            page = pt[b * (PAST_LEN // PAGE_SIZE) + s * PAGES_PER_STEP + j]
            return (layer, page) + (0,) * n_lead
        return pl.BlockSpec(block, imap)
    return [spec(j) for j in range(PAGES_PER_STEP)]


def _idx_walk_kernel(pt_ref, q_ref, w_ref, *refs):
    pages, o_ref = refs[:-1], refs[-1]
    keys = jnp.concatenate([r[0, 0] for r in pages], axis=0)
    d = _dot3_nt(q_ref[0], keys) * (D_IDX ** -0.5)
    o_ref[0, 0] = jnp.sum(jnp.maximum(d, 0.0) * (w_ref[0] * (H_I ** -0.5)), axis=0, keepdims=True)


def _dsa_thresh_kernel(sc_ref, qi_ref, wi_ref, kin_ref, sel_ref, selnew_ref, *, k_sel):
    nb = sc_ref.shape[0]
    snew = []
    for b in range(nb):
        d = jnp.sum(qi_ref[b] * kin_ref[b:b + 1, :], axis=1, keepdims=True) * (D_IDX ** -0.5)
        snew.append(jnp.sum(jnp.maximum(d, 0.0) * (wi_ref[b] * (H_I ** -0.5)), axis=0, keepdims=True))
    snew = jnp.concatenate(snew, axis=0)
    key, knew = _order_key(sc_ref[...]), _order_key(snew)
    count_ge = lambda c: (jnp.sum(jnp.where(key >= c, 1.0, 0.0), axis=1, keepdims=True) + jnp.where(knew >= c, 1.0, 0.0))
    thr = _kth_largest_key(count_ge, nb, k_sel)
    sel_ref[...] = jnp.where((key >= thr) & (sc_ref[...] > NEG * 0.5), 1.0, 0.0)
    selnew_ref[...] = jnp.broadcast_to(jnp.where((knew >= thr) & (snew > NEG * 0.5), 1.0, 0.0), selnew_ref.shape)


def _dsa_walk_kernel(pt_ref, q_ref, sel_ref, selnew_ref, knew_ref, vnew_ref, *refs):
    n = PAGES_PER_STEP
    kpages, vpages = refs[:n], refs[n:2 * n]
    o_ref, m_ref, l_ref, acc_ref = refs[2 * n:]
    s_id = pl.program_id(1)
    row = lax.broadcasted_iota(I32, (H_A, 1), 0)
    first = row < G

    @pl.when(s_id == 0)
    def _():
        m_ref[...] = jnp.full_like(m_ref, NEG)
        l_ref[...] = jnp.zeros_like(l_ref)
        acc_ref[...] = jnp.zeros_like(acc_ref)

    q = q_ref[0]
    k = jnp.concatenate([r[0, 0] for r in kpages], axis=0).astype(BF16)
    v = jnp.concatenate([r[0, 0] for r in vpages], axis=0).astype(BF16)
    col = lax.broadcasted_iota(I32, (1, k.shape[0]), 1)
    mask = jnp.where((col & 1) == jnp.where(first, 0, 1), sel_ref[0, 0], 0.0) > 0.5
    s = jnp.where(mask, _dot_nt(q, k) * SCALE, NEG)
    m_new = jnp.maximum(m_ref[...], jnp.max(s, axis=1, keepdims=True))
    p = jnp.where(mask, jnp.exp(s - m_new), 0.0)
    alpha = jnp.exp(m_ref[...] - m_new)
    l_ref[...] = alpha * l_ref[...] + jnp.sum(p, axis=1, keepdims=True)
    acc_ref[...] = alpha * acc_ref[...] + jnp.dot(p.astype(BF16), v, preferred_element_type=F32)
    m_ref[...] = m_new

    @pl.when(s_id == pl.num_programs(1) - 1)
    def _():
        kn = jnp.where(first, knew_ref[0][:, :DH], knew_ref[0][:, DH:])
        vn = jnp.where(first, vnew_ref[0][:, :DH], vnew_ref[0][:, DH:])
        valid = selnew_ref[0][:, :1] > 0.5
        sn = jnp.where(valid, jnp.sum(q * kn, axis=1, keepdims=True) * SCALE, NEG)
        m_fin = jnp.maximum(m_ref[...], sn)
        pn = jnp.where(valid, jnp.exp(sn - m_fin), 0.0)
        a2 = jnp.exp(m_ref[...] - m_fin)
        l_fin = a2 * l_ref[...] + pn
        o_ref[0] = (a2 * acc_ref[...] + pn * vn) * (1.0 / jnp.maximum(l_fin, 1e-30))


def _dsa_sample(l, z128, z64, zpl, cache_k, cache_v, cache_ki, pt_flat, DB):
    n_pages = PAST_LEN // PAGE_SIZE
    n_steps = n_pages // PAGES_PER_STEP
    span = PAGES_PER_STEP * PAGE_SIZE
    k_sel = min(TOPK_MAX, (PAST_LEN + 1) // 4)
    qi = z64[:DB, :H_I * D_IDX].reshape(DB, H_I, D_IDX)
    wi = zpl[:DB, 1024:1024 + H_I].reshape(DB, H_I, 1)
    kin = z64[:DB, 1024:1024 + D_IDX]
    scores = pl.pallas_call(
        _idx_walk_kernel, name='idx_walk',
        grid_spec=pltpu.PrefetchScalarGridSpec(
            num_scalar_prefetch=1, grid=(DB, n_steps),
            in_specs=[pl.BlockSpec((1, H_I, D_IDX), lambda b, s, pt: (b, 0, 0)), pl.BlockSpec((1, H_I, 1), lambda b, s, pt: (b, 0, 0))]
            + _page_specs((1, 1, PAGE_SIZE, D_IDX), 2, l),
            out_specs=pl.BlockSpec((1, 1, 1, span), lambda b, s, pt: (b, s, 0, 0))),
        out_shape=jax.ShapeDtypeStruct((DB, n_steps, 1, span), F32),
        compiler_params=_params('parallel', 'parallel'),
    )(pt_flat, qi, wi, *([cache_ki] * PAGES_PER_STEP))
    sel, selnew = pl.pallas_call(
        functools.partial(_dsa_thresh_kernel, k_sel=k_sel), name='dsa_thresh',
        out_shape=[jax.ShapeDtypeStruct((DB, PAST_LEN), F32), jax.ShapeDtypeStruct((DB, 128), F32)],
        compiler_params=pltpu.CompilerParams(vmem_limit_bytes=VMEM_LIMIT),
    )(scores.reshape(DB, PAST_LEN), qi, wi, kin)
    qa = z128[:DB, :H_A * DH].reshape(DB, H_A, DH)
    knew = jnp.broadcast_to(z128[:DB, None, 2048:2304], (DB, H_A, KVH_A * DH))
    vnew = jnp.broadcast_to(zpl[:DB, None, 0:256], (DB, H_A, KVH_A * DH))
    selnew = jnp.broadcast_to(selnew[:, None, :], (DB, H_A, 128))
    per_b = lambda shape: pl.BlockSpec(shape, lambda b, s, pt: (b, 0, 0))
    o = pl.pallas_call(
        _dsa_walk_kernel, name='dsa_walk',
        grid_spec=pltpu.PrefetchScalarGridSpec(
            num_scalar_prefetch=1, grid=(DB, n_steps),
            in_specs=[per_b((1, H_A, DH)), pl.BlockSpec((1, 1, 1, KVH_A * span), lambda b, s, pt: (b, s, 0, 0)),
                      per_b((1, H_A, 128)), per_b((1, H_A, KVH_A * DH)), per_b((1, H_A, KVH_A * DH))]
            + _page_specs((1, 1, KVH_A * PAGE_SIZE, DH), 2, l) * 2,
            out_specs=per_b((1, H_A, DH)),
            scratch_shapes=[pltpu.VMEM((H_A, 1), F32), pltpu.VMEM((H_A, 1), F32), pltpu.VMEM((H_A, DH), F32)]),
        out_shape=jax.ShapeDtypeStruct((DB, H_A, DH), F32),
        compiler_params=_params('parallel', 'arbitrary'),
    )(pt_flat, qa, jnp.repeat(sel, KVH_A, axis=1).reshape(DB, n_steps, 1, KVH_A * span), selnew, knew, vnew,
      *([cache_k] * PAGES_PER_STEP), *([cache_v] * PAGES_PER_STEP))
    return o.reshape(DB, H_A * DH)


def _compress_sample(l, cache_k, cache_v, cw, pt_flat, DB):
    pek, w1k, w2k, pev, w1v, w2v = cw
    n_pages = PAST_LEN // PAGE_SIZE
    n_steps = n_pages // PAGES_PER_STEP
    cps = PAGE_SIZE // S_CMP
    nch_step = PAGES_PER_STEP * cps
    const = lambda b, s, pt: (0, 0)
    out_spec = pl.BlockSpec((1, nch_step, KVH_B * DH), lambda b, s, pt: (b, s, 0))
    page_specs = _page_specs((1, 1, KVH_B * PAGE_SIZE, DH), 2, l)
    parts = pl.pallas_call(
        functools.partial(_cmp_partial_kernel, n_src=PAGES_PER_STEP, paged=True), name='cmp_partial_sample',
        grid_spec=pltpu.PrefetchScalarGridSpec(
            num_scalar_prefetch=1, grid=(DB, n_steps),
            in_specs=page_specs + page_specs + _cmp_weight_specs(const),
            out_specs=[out_spec] * 4),
        out_shape=[jax.ShapeDtypeStruct((DB, PAST_LEN // S_CMP, KVH_B * DH), F32)] * 4,
        compiler_params=_params('parallel', 'parallel'),
    )(pt_flat, *([cache_k] * PAGES_PER_STEP), *([cache_v] * PAGES_PER_STEP), pek, w1k.astype(BF16), pev, w1v.astype(BF16))
    return _cmp_finish(parts, w2k, w2v)


def _cmp_select_sample_kernel(q_ref, kc_ref, vc_ref, ov_ref, o_ref, ids_ref, *, n_cmp, n_slc, n_sel):
    n = kc_ref.shape[1]
    q = q_ref[0]
    row = lax.broadcasted_iota(I32, (H_B, 1), 0)
    first = row < G
    kc = kc_ref[0].astype(BF16)
    vc = vc_ref[0].astype(BF16)
    s = jnp.where(first, _dot_nt(q, kc[:, :DH]), _dot_nt(q, kc[:, DH:])) * SCALE
    col = lax.broadcasted_iota(I32, (1, n), 1)
    qpos = PAST_LEN
    pr = _mprobs(s, (col * S_CMP + (L_CMP - 1) <= qpos) & (col < n_cmp))
    pb = pr.astype(BF16)
    o_ref[0] = jnp.where(first, jnp.dot(pb, vc[:, :DH], preferred_element_type=F32), jnp.dot(pb, vc[:, DH:], preferred_element_type=F32))
    psum = jnp.concatenate([jnp.sum(jnp.where((row >= kv * G) & (row < (kv + 1) * G), pr, 0.0), axis=0, keepdims=True)
                            for kv in range(KVH_B)] + [jnp.zeros((8 - KVH_B, n), F32)], axis=0)
    ph, plo = _split(psum)
    imp = jnp.dot(ph, ov_ref[...], preferred_element_type=F32) + jnp.dot(plo, ov_ref[...], preferred_element_type=F32)
    j = lax.broadcasted_iota(I32, imp.shape, 1)
    jf = j.astype(F32)
    jt = qpos // L_SLC
    forced = (j == 0) | (j == jt) | (j == jt - 1)
    sc = jnp.where((j > jt) | (j >= n_slc), NEG, jnp.where(forced, FORCED, imp))
    lane = lax.broadcasted_iota(I32, (8, 128), 1)
    ids = jnp.zeros((8, 128), F32)
    for r in range(n_sel):
        m = jnp.max(sc, axis=1, keepdims=True)
        pick = jnp.min(jnp.where(sc == m, jf, 1e9), axis=1, keepdims=True)
        ids = jnp.where(lane == r, pick, ids)
        sc = jnp.where(jf == pick, -jnp.inf, sc)
    ids_ref[0] = ids.astype(I32)


def _slc_sample_kernel(ids_ref, pt_ref, q_ref, knew_ref, vnew_ref, kblk_ref, vblk_ref, o_ref, kbuf, vbuf, *, n_sel, n_past):
    b, kv, r = pl.program_id(0), pl.program_id(1), pl.program_id(2)
    base = (b * KVH_B + kv) * n_sel
    bid = ids_ref[base + r]
    nrow = KVH_B * L_SLC
    row = lax.broadcasted_iota(I32, (nrow, 1), 0)
    past = bid < n_past
    dst = pl.ds(pl.multiple_of(r * nrow, nrow), nrow)
    kbuf[dst, :] = jnp.where(past, kblk_ref[0, 0, 0], jnp.where(row == kv, knew_ref[0, 0], 0.0))
    vbuf[dst, :] = jnp.where(past, vblk_ref[0, 0, 0], jnp.where(row == kv, vnew_ref[0, 0], 0.0))

    @pl.when(r == n_sel - 1)
    def _():
        lane = lax.broadcasted_iota(I32, (1, n_sel * nrow), 1)
        tpos = (lane % nrow) // KVH_B
        for rr in range(n_sel):
            tpos = tpos + jnp.where(lane // nrow == rr, ids_ref[base + rr] * L_SLC, 0)
        mask = jnp.where(lane % KVH_B == kv, tpos, PAST_LEN + 1) <= PAST_LEN
        s = _dot_nt(q_ref[0, 0], kbuf[...]) * SCALE
        o_ref[0, 0] = _dot(_mprobs(s, mask), vbuf[...])


def _win_sample_kernel(q_ref, kb_ref, vb_ref, kn_ref, vn_ref, o_ref, ko_ref, vo_ref):
    W = kb_ref.shape[2]
    q = q_ref[0]
    rowq = lax.broadcasted_iota(I32, (H_B, 1), 0)
    first = rowq < G
    kb, vb = kb_ref[0, 0], vb_ref[0, 0]
    kn, vn = kn_ref[0], vn_ref[0]
    s = jnp.where(first, _dot_nt(q, kb[:, :DH]), _dot_nt(q, kb[:, DH:])) * SCALE
    col = lax.broadcasted_iota(I32, (1, W), 1)
    mask = (PAST_LEN - W + col) > (PAST_LEN - WINDOW)
    knh = jnp.where(first, kn[:, :DH], kn[:, DH:])
    vnh = jnp.where(first, vn[:, :DH], vn[:, DH:])
    sn = jnp.sum(q * knh, axis=1, keepdims=True) * SCALE
    s = jnp.where(mask, s, NEG)
    m = jnp.maximum(jnp.max(s, axis=1, keepdims=True), sn)
    p = jnp.where(mask, jnp.exp(s - m), 0.0)
    pn = jnp.exp(sn - m)
    inv = 1.0 / jnp.maximum(jnp.sum(p, axis=1, keepdims=True) + pn, 1e-30)
    pb = (p * inv).astype(BF16)
    vbb = vb.astype(BF16)
    pv = jnp.where(first, jnp.dot(pb, vbb[:, :DH], preferred_element_type=F32), jnp.dot(pb, vbb[:, DH:], preferred_element_type=F32))
    o_ref[0] = pv + (pn * inv) * vnh
    roww = lax.broadcasted_iota(I32, (W, 1), 0)
    ko_ref[0] = jnp.where(roww == W - 1, kn[:1, :], pltpu.roll(kb, W - 1, 0))
    vo_ref[0] = jnp.where(roww == W - 1, vn[:1, :], pltpu.roll(vb, W - 1, 0))


def _nsa_sample(l, z128, zpl, cache_ck, cache_cv, cache_sk, cache_sv, buf_k, buf_v, cw, pt_flat, DB):
    kc, vc = _compress_sample(l, cache_ck, cache_cv, cw, pt_flat, DB)
    n = kc.shape[1]
    L = PAST_LEN + 1
    n_cmp = L // S_CMP - L_CMP // S_CMP + 1
    n_slc = -(-L // L_SLC)
    n_sel = min(N_SEL_MAX, n_slc)
    n_past = PAST_LEN // L_SLC
    cols = -(-n_slc // 128) * 128
    qb = z128[:DB, 1024:2048].reshape(DB, H_B, DH)
    per_b = lambda shape: pl.BlockSpec(shape, lambda b: (b,) + (0,) * (len(shape) - 1))
    o_cmp, ids = pl.pallas_call(
        functools.partial(_cmp_select_sample_kernel, n_cmp=n_cmp, n_slc=n_slc, n_sel=n_sel), grid=(DB,), name='cmp_select_sample',
        in_specs=[per_b((1, H_B, DH)), per_b((1, n, 256)), per_b((1, n, 256)), pl.BlockSpec((n, cols), lambda b: (0, 0))],
        out_specs=[per_b((1, H_B, DH)), per_b((1, 8, 128))],
        out_shape=[jax.ShapeDtypeStruct((DB, H_B, DH), F32), jax.ShapeDtypeStruct((DB, 8, 128), I32)],
        compiler_params=_params('parallel'),
    )(qb, kc, vc, _overlap_np(n_cmp, n_slc, n, cols))
    ids_flat = ids[:, :KVH_B, :n_sel].reshape(-1)
    q4 = jnp.pad(qb.reshape(DB, KVH_B, G, DH), ((0, 0), (0, 0), (0, 8 - G), (0, 0)))
    knew = z128[:DB, 2560:2816].reshape(DB, KVH_B, 1, DH)
    vnew = zpl[:DB, 512:768].reshape(DB, KVH_B, 1, DH)
    bpp = PAGE_SIZE // L_SLC
    n_pages = PAST_LEN // PAGE_SIZE

    def blk_map(b, kv, r, ids_s, pt):
        bid = jnp.minimum(ids_s[(b * KVH_B + kv) * n_sel + r], n_past - 1)
        return (l, pt[b * n_pages + bid // bpp], bid % bpp, 0, 0)

    per_bk = lambda shape: pl.BlockSpec(shape, lambda b, kv, r, ids_s, pt: (b, kv, 0, 0))
    nrow = KVH_B * L_SLC
    view = lambda c: c.reshape(c.shape[0], c.shape[1], bpp, nrow, DH)
    o_slc = pl.pallas_call(
        functools.partial(_slc_sample_kernel, n_sel=n_sel, n_past=n_past), name='slc_sample',
        grid_spec=pltpu.PrefetchScalarGridSpec(
            num_scalar_prefetch=2, grid=(DB, KVH_B, n_sel),
            in_specs=[per_bk((1, 1, 8, DH)), per_bk((1, 1, 1, DH)), per_bk((1, 1, 1, DH)),
                      pl.BlockSpec((1, 1, 1, nrow, DH), blk_map), pl.BlockSpec((1, 1, 1, nrow, DH), blk_map)],
            out_specs=per_bk((1, 1, 8, DH)),
            scratch_shapes=[pltpu.VMEM((n_sel * nrow, DH), F32)] * 2),
        out_shape=jax.ShapeDtypeStruct((DB, KVH_B, 8, DH), F32),
        compiler_params=_params('parallel', 'parallel', 'arbitrary'),
    )(ids_flat, pt_flat, q4, knew, vnew, view(cache_sk), view(cache_sv))
    W = buf_k.shape[2]
    kn8 = jnp.broadcast_to(z128[:DB, None, 2816:3072], (DB, 8, 256))
    vn8 = jnp.broadcast_to(zpl[:DB, None, 768:1024], (DB, 8, 256))
    bufspec = pl.BlockSpec((1, 1, W, 256), lambda b: (l, b, 0, 0))
    o_win, nk, nv = pl.pallas_call(
        _win_sample_kernel, grid=(DB,), name='win_sample',
        in_specs=[per_b((1, H_B, DH)), bufspec, bufspec, per_b((1, 8, 256)), per_b((1, 8, 256))],
        out_specs=[per_b((1, H_B, DH)), per_b((1, W, 256)), per_b((1, W, 256))],
        out_shape=[jax.ShapeDtypeStruct((DB, H_B, DH), F32)] + [jax.ShapeDtypeStruct((DB, W, 256), F32)] * 2,
        compiler_params=_params('parallel'),
    )(qb, buf_k.reshape(buf_k.shape[0], DB, W, 256), buf_v.reshape(buf_v.shape[0], DB, W, 256), kn8, vn8)
    gb = jax.nn.sigmoid(zpl[:DB, 1024 + GATE_OFF:1024 + GATE_OFF + 3 * H_B]).reshape(DB, H_B, 3)
    o_b = gb[..., 0:1] * o_cmp + gb[..., 1:2] * o_slc[:, :, :G].reshape(DB, H_B, DH) + gb[..., 2:3] * o_win
    return o_b.reshape(DB, H_B * DH), nk, nv


def kernel(x_prompt, x_sample, cache_dsa_k, cache_dsa_v, cache_idx_k, cache_cmp_k, cache_cmp_v, cache_slc_k, cache_slc_v, state_win_k, state_win_v, page_table, norm_mix, w_in, cmp_pe_k, cmp_w1_k, cmp_w2_k, cmp_pe_v, cmp_w1_v, cmp_w2_v, w_up_a, w_up_b, w_o, norm_ffn, dense_w_gate, dense_w_up, dense_w_down, moe_w_router, moe_w_gate, moe_w_up, moe_w_down, norm_final):
    B, S, D = x_prompt.shape
    DB, T, _ = x_sample.shape
    depth = w_in.shape[0]
    assert T == 1 and D == D_MODEL and page_table.shape == (DB, PAST_LEN // PAGE_SIZE)
    wp = min(WINDOW, S)
    xp = x_prompt.reshape(B * S, D)
    xs = jnp.pad(x_sample.reshape(DB, D), ((0, SAMPLE_ROWS - DB), (0, 0)))
    pt_flat = page_table.reshape(-1).astype(I32)
    pos_p = jnp.tile(jnp.arange(S), B)
    pos_s = jnp.full((SAMPLE_ROWS,), PAST_LEN)
    tabs_p = (_rope_tables(pos_p, ROT_DIM, DH), _rope_tables(pos_p, ROT_IDX, D_IDX))
    tabs_s = (_rope_tables(pos_s, ROT_DIM, DH), _rope_tables(pos_s, ROT_IDX, D_IDX))
    rows_view = lambda c: c.reshape(c.shape[0], c.shape[1], -1, c.shape[-1])
    c_dsa_k, c_dsa_v, c_idx, c_cmp_k, c_cmp_v, c_slc_k, c_slc_v = map(
        rows_view, (cache_dsa_k, cache_dsa_v, cache_idx_k, cache_cmp_k, cache_cmp_v, cache_slc_k, cache_slc_v))
    newp, news = [], []
    for l in range(depth):
        cw = (cmp_pe_k[l], cmp_w1_k[l], cmp_w2_k[l], cmp_pe_v[l], cmp_w1_v[l], cmp_w2_v[l])
        wsplit = _split_w_in(w_in[l])
        wa, wb, wo = w_up_a[l].astype(BF16), w_up_b[l].astype(BF16), w_o[l].astype(BF16)

        z128, z64, zpl, zg = _project(_rms(xp, norm_mix[l], BF16), wsplit, *tabs_p)
        z128b, z64b, zplb = (z.reshape(B, S, -1) for z in (z128, z64, zpl))
        o_a = _dsa_prompt(z128b, z64b, zplb, B, S)
        kc, vc = _compress_prompt(z128b, zplb, cw, B, S)
        o_b = _nsa_prompt(z128b, zplb, kc, vc, B, S)
        xp = _merge(xp, o_a.reshape(B * S, -1), o_b.reshape(B * S, -1), zg, wa, wb, wo)
        kv4 = lambda z, c: z[:, :, c:c + 256].reshape(B, S, 2, DH)
        newp.append((kv4(z128b, 2048), kv4(zplb, 0), z64b[:, :, 1024:1024 + D_IDX], kv4(z128b, 2304), kv4(zplb, 256),
                     kv4(z128b, 2560), kv4(zplb, 512), kv4(z128b, 2816)[:, S - wp:], kv4(zplb, 768)[:, S - wp:]))

        y128, y64, ypl, yg = _project(_rms(xs, norm_mix[l], BF16), wsplit, *tabs_s)
        o_a = _dsa_sample(l, y128, y64, ypl, c_dsa_k, c_dsa_v, c_idx, pt_flat, DB)
        o_b, win_k, win_v = _nsa_sample(l, y128, ypl, c_cmp_k, c_cmp_v, c_slc_k, c_slc_v, state_win_k, state_win_v, cw, pt_flat, DB)
        pad = lambda o: jnp.pad(o, ((0, SAMPLE_ROWS - DB), (0, 0)))
        xs = _merge(xs, pad(o_a), pad(o_b), yg, wa, wb, wo)
        s4 = lambda z, c: z[:DB, c:c + 256].reshape(DB, 1, 2, DH)
        news.append((s4(y128, 2048), s4(ypl, 0), y64[:DB, 1024:1024 + D_IDX].reshape(DB, 1, D_IDX), s4(y128, 2304), s4(ypl, 256),
                     s4(y128, 2560), s4(ypl, 512), win_k.reshape(DB, -1, 2, DH), win_v.reshape(DB, -1, 2, DH)))

        i = l // 2
        if l % 2 == 0:
            ws = (dense_w_gate[i].astype(BF16), dense_w_up[i].astype(BF16), dense_w_down[i].astype(BF16))
            xp = _dense_ffn(xp, norm_ffn[l], *ws)
            xs = _dense_ffn(xs, norm_ffn[l], *ws)
        else:
            xp, xs = _moe(xp, xs, norm_ffn[l], moe_w_router[i], moe_w_gate[i].astype(BF16), moe_w_up[i].astype(BF16),
                          moe_w_down[i].astype(BF16))

    y_prompt = _rms(xp, norm_final, F32).reshape(B, S, D)
    y_sample = _rms(xs, norm_final, F32)[:DB].reshape(DB, 1, D)
    stack = lambda items: [jnp.stack([it[n] for it in items], axis=0) for n in range(9)]
    return (y_prompt, y_sample, *stack(newp), *stack(news))
```

```python
import functools

import numpy as np
import jax
import jax.numpy as jnp
from jax import lax
from jax.experimental import pallas as pl
from jax.experimental.pallas import tpu as pltpu

F32 = jnp.float32
BF16 = jnp.bfloat16
I32 = jnp.int32

D_MODEL = 2048
PAST_LEN = 16384
PAGE_SIZE = 128
DH = 128
ROT_DIM = DH // 4
ROPE_THETA = 500000.0
H_A = 8
KVH_A = 2
H_I = 16
D_IDX = 64
ROT_IDX = D_IDX // 4
TOPK_MAX = 256
H_B = 8
KVH_B = 2
G = 4
L_CMP = 32
S_CMP = 16
L_SLC = 64
N_SEL_MAX = 16
WINDOW = 512
N_EXP = 8
Q_BLK = 128
FFN_TM = 512
KEY_CLS = 512
EPS = 1e-6
NEG = -1e30
FORCED = 1e6
INT_MIN = -2 ** 31
SCALE = DH ** -0.5
PAGES_PER_STEP = 16
SAMPLE_ROWS = 16
VMEM_LIMIT = 48 * 1024 * 1024

_IN_SPLITS = (
    ('q_a', H_A * DH), ('k_a', KVH_A * DH), ('v_a', KVH_A * DH),
    ('q_i', H_I * D_IDX), ('k_i', D_IDX), ('w_i', H_I),
    ('q_b', H_B * DH),
    ('k_cmp', KVH_B * DH), ('v_cmp', KVH_B * DH),
    ('k_slc', KVH_B * DH), ('v_slc', KVH_B * DH),
    ('k_win', KVH_B * DH), ('v_win', KVH_B * DH),
    ('g_b', 3 * H_B), ('g_mix', 2 * D_MODEL),
)
_GROUP_ROPE128 = ('q_a', 'q_b', 'k_a', 'k_cmp', 'k_slc', 'k_win')
_GROUP_ROPE64 = ('q_i', 'k_i')
_GROUP_PLAIN = ('v_a', 'v_cmp', 'v_slc', 'v_win', 'w_i', 'g_b')
W_R64 = 1280
W_PLAIN = 1280
GATE_OFF = H_I


def _params(*sem):
    return pltpu.CompilerParams(dimension_semantics=sem, vmem_limit_bytes=VMEM_LIMIT)


def _dot(a, b):
    return jnp.dot(a.astype(BF16), b.astype(BF16), preferred_element_type=F32)


def _dg_nt(a, b):
    return lax.dot_general(a, b, (((1,), (1,)), ((), ())), preferred_element_type=F32)


def _dot_nt(a, b):
    return _dg_nt(a.astype(BF16), b.astype(BF16))


def _split(a):
    hi = a.astype(BF16)
    lo = (a - hi.astype(F32)).astype(BF16)
    return hi, lo


def _dot3_nt(a, b):
    ah, al = _split(a)
    bh, bl = _split(b)
    return _dg_nt(ah, bh) + _dg_nt(ah, bl) + _dg_nt(al, bh)


def _dot3(a, b):
    ah, al = _split(a)
    bh, bl = _split(b)
    d = lambda x, y: jnp.dot(x, y, preferred_element_type=F32)
    return d(ah, bh) + d(ah, bl) + d(al, bh)


def _dot3w(a, wh, wl):
    ah, al = _split(a)
    d = lambda x, y: jnp.dot(x, y, preferred_element_type=F32)
    return d(ah, wh) + d(ah, wl) + d(al, wh)


def _hi_lo(w):
    hi = w.astype(BF16)
    return hi, (w - hi.astype(F32)).astype(BF16)


def _mprobs(s, mask):
    s = jnp.where(mask, s, NEG)
    m = jnp.max(s, axis=-1, keepdims=True)
    p = jnp.where(mask, jnp.exp(s - m), 0.0)
    return p * (1.0 / jnp.maximum(jnp.sum(p, axis=-1, keepdims=True), 1e-30))


def _order_key(x):
    b = lax.bitcast_convert_type(x + 0.0, I32)
    return jnp.where(b < 0, b ^ jnp.int32(0x7FFFFFFF), b)


def _kth_largest_key(count_ge, rows, k, n_bits=32):
    def body(it, lo):
        cand = lo + jnp.left_shift(jnp.int32(1), 31 - it)
        return jnp.where(count_ge(cand) >= k, cand, lo)
    return lax.fori_loop(0, n_bits, body, jnp.full((rows, 1), INT_MIN, I32))


def _rms_kernel(x_ref, g_ref, o_ref):
    x = x_ref[...]
    y = x * lax.rsqrt(jnp.mean(x * x, axis=-1, keepdims=True) + EPS)
    o_ref[...] = (y * g_ref[...]).astype(o_ref.dtype)


def _rms(x, g, out_dtype):
    M, D = x.shape
    tm = min(M, 512)
    return pl.pallas_call(
        _rms_kernel, grid=(M // tm,), name='rms',
        in_specs=[pl.BlockSpec((tm, D), lambda i: (i, 0)), pl.BlockSpec((1, D), lambda i: (0, 0))],
        out_specs=pl.BlockSpec((tm, D), lambda i: (i, 0)),
        out_shape=jax.ShapeDtypeStruct((M, D), out_dtype),
        compiler_params=_params('parallel'),
    )(x, g.reshape(1, D))


def _mm_kernel(*refs, mode, shift, out_kinds, precise):
    if precise:
        acc = _dot3w(refs[0][...], refs[1][...], refs[2][...])
        refs = refs[1:]
    else:
        acc = jnp.dot(refs[0][...], refs[1][...], preferred_element_type=F32)
    if mode == 'rope':
        c_ref, s1_ref, s2_ref = refs[2:5]
        tn = acc.shape[1]
        rep = tn // 128
        til = lambda r: jnp.concatenate([r[...]] * rep, axis=1)
        acc = (acc * til(c_ref) + pltpu.roll(acc, tn - shift, 1) * til(s1_ref)
               + pltpu.roll(acc, shift, 1) * til(s2_ref))
    elif mode == 'sigmoid':
        acc = jax.nn.sigmoid(acc)
    elif mode == 'res':
        acc = acc + refs[2][...]
    for kind, o_ref in zip(out_kinds, refs[len(refs) - len(out_kinds):]):
        if kind == 'lo':
            o_ref[...] = (acc - acc.astype(BF16).astype(F32)).astype(BF16)
        else:
            o_ref[...] = acc.astype(o_ref.dtype)


_KIND_DTYPE = {'f32': F32, 'bf16': BF16, 'lo': BF16}


def _mm(a, b, tn, mode='none', shift=0, extra=(), out_kinds=('f32',), name='mm'):
    precise = isinstance(b, tuple)
    ws = b if precise else (b,)
    out_dtypes = tuple(_KIND_DTYPE[k] for k in out_kinds)
    M, K = a.shape
    N = ws[0].shape[1]
    tm = min(M, 1024)
    in_specs = [pl.BlockSpec((tm, K), lambda i, j: (i, 0))] + [pl.BlockSpec((K, tn), lambda i, j: (0, j))] * len(ws)
    if mode == 'rope':
        in_specs += [pl.BlockSpec((tm, 128), lambda i, j: (i, 0))] * 3
    elif mode == 'res':
        in_specs += [pl.BlockSpec((tm, tn), lambda i, j: (i, j))]
    outs = pl.pallas_call(
        functools.partial(_mm_kernel, mode=mode, shift=shift, out_kinds=out_kinds, precise=precise),
        grid=(M // tm, N // tn), name=name,
        in_specs=in_specs, out_specs=[pl.BlockSpec((tm, tn), lambda i, j: (i, j))] * len(out_dtypes),
        out_shape=[jax.ShapeDtypeStruct((M, N), dt) for dt in out_dtypes],
        compiler_params=_params('parallel', 'parallel'),
    )(a, *ws, *extra)
    return outs[0] if len(out_dtypes) == 1 else tuple(outs)


def _rope_tables(pos, rot, period):
    half = rot // 2
    inv = jnp.power(jnp.float32(ROPE_THETA), -jnp.arange(half, dtype=F32) / half)
    ang = pos.astype(F32)[:, None] * inv[None, :]
    cos, sin = jnp.cos(ang), jnp.sin(ang)
    T = pos.shape[0]
    z = jnp.zeros((T, period - rot), F32)
    zh = jnp.zeros((T, half), F32)
    c = jnp.concatenate([cos, cos, jnp.ones((T, period - rot), F32)], axis=1)
    s1 = jnp.concatenate([-sin, zh, z], axis=1)
    s2 = jnp.concatenate([zh, sin, z], axis=1)
    rep = 128 // period
    return tuple(jnp.tile(t, (1, rep)) for t in (c, s1, s2))


def _split_w_in(w):
    parts, off = {}, 0
    for name, width in _IN_SPLITS:
        parts[name] = w[:, off:off + width]
        off += width
    cat = lambda names, total: jnp.pad(
        jnp.concatenate([parts[n] for n in names], axis=1),
        ((0, 0), (0, total - sum(parts[n].shape[1] for n in names))))
    zeros_ki = jnp.zeros_like(parts['k_i'])
    w64 = jnp.concatenate([parts['q_i'], parts['k_i'], zeros_ki, zeros_ki, parts['k_i']], axis=1)
    assert w64.shape[1] == W_R64
    return tuple(_hi_lo(g) for g in (cat(_GROUP_ROPE128, 3072), w64, cat(_GROUP_PLAIN, W_PLAIN), parts['g_mix']))


def _project(h, wsplit, tabs128, tabs64, precise=False):
    w128, w64, wpl, wg = (w if precise else w[0] for w in wsplit)
    both = ('f32',) if precise else ('f32', 'bf16')
    z128 = _mm(h, w128, 512, 'rope', ROT_DIM // 2, tabs128, out_kinds=both, name='proj_rope128')
    z64 = _mm(h, w64, 640, 'rope', ROT_IDX // 2, tabs64, out_kinds=both + (() if precise else ('lo',)), name='proj_rope64')
    zpl = _mm(h, wpl, 640, out_kinds=both, name='proj_plain')
    zg = _mm(h, wg, 512, 'sigmoid', name='proj_gmix')
    return z128, z64, zpl, zg


def _stack_heads(q_ref, kv):
    return jnp.concatenate([q_ref[0, :, (kv * G + g) * DH:(kv * G + g + 1) * DH] for g in range(G)], axis=0).astype(BF16)


def _masked_attend(q, k, v, mask):
    s = _dg_nt(q, k) * SCALE
    p = jnp.concatenate([_mprobs(s[g * Q_BLK:(g + 1) * Q_BLK], mask).astype(BF16) for g in range(G)], axis=0)
    return jnp.dot(p, v, preferred_element_type=F32)


def _dsa_prompt_body(qa_ref, ka_ref, va_ref, qih_ref, qil_ref, kih_ref, kil_ref, ki2h_ref, ki2l_ref, wi_ref, o_ref,
                     *, k_sel, ext):
    i = pl.program_id(1)
    w = wi_ref[0][:, :H_I] * (H_I ** -0.5)
    npair = H_I * D_IDX // 128
    pairs = lambda r: jnp.concatenate([r[0, :, j * 128:(j + 1) * 128] for j in range(npair)], axis=0)
    qh = pairs(qih_ref)
    q3 = jnp.concatenate([qh, qh, pairs(qil_ref)], axis=1)
    parts = []
    for c0 in range(0, ext, KEY_CLS):
        rows = lambda r1, r2: jnp.concatenate([r1[0, c0:c0 + KEY_CLS, :], r2[0, c0:c0 + KEY_CLS, :]], axis=0)
        kh = rows(kih_ref, ki2h_ref)
        d = _dg_nt(q3, jnp.concatenate([kh, rows(kil_ref, ki2l_ref), kh], axis=1)) * (D_IDX ** -0.5)
        sc = None
        for j in range(npair):
            dj = d[j * Q_BLK:(j + 1) * Q_BLK]
            t = (jnp.maximum(dj[:, :KEY_CLS], 0.0) * w[:, 2 * j:2 * j + 1]
                 + jnp.maximum(dj[:, KEY_CLS:], 0.0) * w[:, 2 * j + 1:2 * j + 2])
            sc = t if sc is None else sc + t
        parts.append(sc)
    score = jnp.concatenate(parts, axis=1)
    qpos = i * Q_BLK + lax.broadcasted_iota(I32, (Q_BLK, 1), 0)
    kpos = lax.broadcasted_iota(I32, (1, ext), 1)
    masked = jnp.where(kpos <= qpos, score, NEG)
    key = jnp.where(masked > NEG * 0.5, _order_key(masked), INT_MIN)
    count_ge = lambda c: jnp.sum(jnp.where(key >= c, 1.0, 0.0), axis=1, keepdims=True)
    n_it = jnp.where((i + 1) * Q_BLK > k_sel, 32, 0)
    thr = _kth_largest_key(count_ge, Q_BLK, k_sel, n_it)
    sel = key >= jnp.maximum(thr, INT_MIN + 1)
    for kv in range(KVH_A):
        sl = slice(kv * DH, (kv + 1) * DH)
        o = _masked_attend(_stack_heads(qa_ref, kv), ka_ref[0, :ext, sl], va_ref[0, :ext, sl], sel)
        for g in range(G):
            o_ref[0, :, (kv * G + g) * DH:(kv * G + g + 1) * DH] = o[g * Q_BLK:(g + 1) * Q_BLK]


def _by_key_class(body, S, **kw):
    def kern(*refs):
        i = pl.program_id(1)
        for c in range(S // KEY_CLS):
            pl.when(i // (KEY_CLS // Q_BLK) == c)(functools.partial(body, *refs, ext=(c + 1) * KEY_CLS, **kw))
    return kern


def _dsa_prompt(z128h, z64h, z64l, zplh, zpl, B, S):
    k_sel = min(TOPK_MAX, S // 4)
    blk = lambda w, f: pl.BlockSpec((1, Q_BLK, w), f)
    full = lambda w, c: pl.BlockSpec((1, S, w), lambda b, i: (b, 0, c))
    return pl.pallas_call(
        _by_key_class(_dsa_prompt_body, S, k_sel=k_sel), grid=(B, S // Q_BLK), name='dsa_prompt',
        in_specs=[blk(1024, lambda b, i: (b, i, 0)), full(256, 8), full(256, 0),
                  blk(1024, lambda b, i: (b, i, 0)), blk(1024, lambda b, i: (b, i, 0)),
                  full(128, 8), full(128, 8), full(128, 9), full(128, 9), blk(128, lambda b, i: (b, i, 8))],
        out_specs=blk(1024, lambda b, i: (b, i, 0)),
        out_shape=jax.ShapeDtypeStruct((B, S, H_A * DH), F32),
        compiler_params=_params('parallel', 'parallel'),
    )(z128h, z128h, zplh, z64h, z64l, z64h, z64l, z64h, z64l, zpl)


def _cmp_partial_kernel(*refs, n_src, paged, precise):
    refs = refs[len(refs) - (2 * n_src + 10):]
    rw = KVH_B * DH
    mm = _dot3w if precise else (lambda x, wh, wl: _dot(x, wh))

    def rows_of(r, p, kvh):
        if paged:
            return r[0, 0, pl.ds(KVH_B * p + kvh, PAGE_SIZE // S_CMP, stride=KVH_B * S_CMP), :]
        return r[0, :, p * rw + kvh * DH:p * rw + (kvh + 1) * DH]
    ksrc, vsrc = refs[:n_src], refs[n_src:2 * n_src]
    pek, w1kh, w1kl, pev, w1vh, w1vl, ak, bk, av, bv = refs[2 * n_src:]
    half = L_CMP // 2
    for srcs, pe_ref, wh_ref, wl_ref, a_ref, b_ref in ((ksrc, pek, w1kh, w1kl, ak, bk), (vsrc, pev, w1vh, w1vl, av, bv)):
        for kvh in range(KVH_B):
            acc_a = acc_b = None
            for p in range(half):
                rows = jnp.concatenate([rows_of(r, p, kvh) for r in srcs], axis=0)
                sa, sb = slice(p * DH, (p + 1) * DH), slice((half + p) * DH, (half + p + 1) * DH)
                da = mm(rows + pe_ref[p:p + 1, :], wh_ref[sa, :], wl_ref[sa, :])
                db = mm(rows + pe_ref[half + p:half + p + 1, :], wh_ref[sb, :], wl_ref[sb, :])
                acc_a = da if acc_a is None else acc_a + da
                acc_b = db if acc_b is None else acc_b + db
            a_ref[0, :, kvh * DH:(kvh + 1) * DH] = acc_a
            b_ref[0, :, kvh * DH:(kvh + 1) * DH] = acc_b


def _cmp_finish_kernel(ak_ref, bk_ref, av_ref, bv_ref, w2k_ref, w2v_ref, kc_ref, vc_ref, *, precise):
    n = ak_ref.shape[1]
    mm = _dot3 if precise else _dot
    for a_ref, b_ref, w2_ref, o_ref in ((ak_ref, bk_ref, w2k_ref, kc_ref), (av_ref, bv_ref, w2v_ref, vc_ref)):
        for kvh in range(KVH_B):
            sl = slice(kvh * DH, (kvh + 1) * DH)
            h = a_ref[0, :, sl] + pltpu.roll(b_ref[0, :, sl], n - 1, 0)
            o_ref[0, :, sl] = mm(jax.nn.gelu(h), w2_ref[...])


def _cmp_finish(parts, w2k, w2v, precise):
    Bt, n, _ = parts[0].shape
    spec = pl.BlockSpec((1, n, KVH_B * DH), lambda b: (b, 0, 0))
    wspec = pl.BlockSpec((DH, DH), lambda b: (0, 0))
    return pl.pallas_call(
        functools.partial(_cmp_finish_kernel, precise=precise), grid=(Bt,), name='cmp_finish',
        in_specs=[spec] * 4 + [wspec] * 2, out_specs=[spec] * 2,
        out_shape=[jax.ShapeDtypeStruct((Bt, n, KVH_B * DH), F32)] * 2,
        compiler_params=_params('parallel'),
    )(*parts, w2k, w2v)


def _cmp_weight_specs(imap):
    w1 = pl.BlockSpec((L_CMP * DH, DH), imap)
    return [pl.BlockSpec((L_CMP, DH), imap), w1, w1] * 2


def _compress_prompt(z128, zpl, cw, B, S):
    pek, w1k, w2k, pev, w1v, w2v = cw
    nch = S // S_CMP
    const = lambda b: (0, 0)
    out_spec = pl.BlockSpec((1, nch, KVH_B * DH), lambda b: (b, 0, 0))
    src_spec = pl.BlockSpec((1, nch, S_CMP * 256), lambda b: (b, 0, 0))
    chunked = lambda z, c: z[:, :, c:c + 256].reshape(B, nch, S_CMP * 256)
    parts = pl.pallas_call(
        functools.partial(_cmp_partial_kernel, n_src=1, paged=False, precise=False), grid=(B,), name='cmp_partial_prompt',
        in_specs=[src_spec, src_spec] + _cmp_weight_specs(const),
        out_specs=[out_spec] * 4,
        out_shape=[jax.ShapeDtypeStruct((B, nch, KVH_B * DH), F32)] * 4,
        compiler_params=_params('parallel'),
    )(chunked(z128, 2304), chunked(zpl, 256), pek, *_hi_lo(w1k), pev, *_hi_lo(w1v))
    return _cmp_finish(parts, w2k, w2v, precise=False)


def _nsa_prompt_body(qb_ref, kc_ref, vc_ref, ks_ref, vs_ref, kw_ref, vw_ref, gt_ref, ov_ref, ex_ref, o_ref,
                     *, n_cmp, n_slc, n_sel, ext):
    i = pl.program_id(1)
    S = ks_ref.shape[1]
    wspan = WINDOW + Q_BLK
    qpos = i * Q_BLK + lax.broadcasted_iota(I32, (Q_BLK, 1), 0)
    lane = lax.broadcasted_iota(I32, (1, 128), 1)
    kpos = lax.broadcasted_iota(I32, (1, ext), 1)
    causal = kpos <= qpos
    vis = (lane * S_CMP + (L_CMP - 1) <= qpos) & (lane < n_cmp)
    jt = qpos // L_SLC
    forced = (lane == 0) | (lane == jt) | (lane == jt - 1)
    wstart = pl.multiple_of(jnp.clip(i * Q_BLK - WINDOW, 0, S - wspan), Q_BLK)
    wpos = wstart + lax.broadcasted_iota(I32, (1, wspan), 1)
    wmask = (wpos <= qpos) & (wpos > qpos - WINDOW)
    gates = jax.nn.sigmoid(gt_ref[0])
    for kv in range(KVH_B):
        sl = slice(kv * DH, (kv + 1) * DH)
        q = _stack_heads(qb_ref, kv)
        s_cmp = _dg_nt(q, kc_ref[0, :, sl].astype(BF16)) * SCALE
        prs = [_mprobs(s_cmp[g * Q_BLK:(g + 1) * Q_BLK], vis) for g in range(G)]
        o_cmp = _dot(jnp.concatenate(prs, axis=0), vc_ref[0, :, sl])
        imp = jnp.dot(jnp.concatenate(_split(prs[0] + prs[1] + prs[2] + prs[3]), axis=1), ov_ref[...],
                      preferred_element_type=F32)
        sc = jnp.where(lane > jt, NEG, jnp.where(forced, FORCED, imp))
        rank = jnp.zeros((Q_BLK, 128), F32)
        for j2 in range(n_slc):
            col = sc[:, j2:j2 + 1]
            rank = rank + jnp.where(col > sc, 1.0, jnp.where((col == sc) & (lane > j2), 1.0, 0.0))
        sel = jnp.where((rank < n_sel) & (sc > NEG * 0.5), 1.0, 0.0).astype(BF16)
        tokmask = jnp.where(causal, jnp.dot(sel, ex_ref[:, :ext], preferred_element_type=F32), 0.0) > 0.5
        o_slc = _masked_attend(q, ks_ref[0, :ext, sl], vs_ref[0, :ext, sl], tokmask)
        o_win = _masked_attend(q, kw_ref[0, pl.ds(wstart, wspan), sl], vw_ref[0, pl.ds(wstart, wspan), sl], wmask)
        for g in range(G):
            hd = kv * G + g
            rows = slice(g * Q_BLK, (g + 1) * Q_BLK)
            c0 = GATE_OFF + hd * 3
            o_ref[0, :, hd * DH:(hd + 1) * DH] = (gates[:, c0:c0 + 1] * o_cmp[rows] + gates[:, c0 + 1:c0 + 2] * o_slc[rows]
                                                  + gates[:, c0 + 2:c0 + 3] * o_win[rows])


def _overlap_np(nc, nslc, rows, cols):
    cs = np.arange(rows)[:, None] * S_CMP
    js = np.arange(cols)[None, :] * L_SLC
    ov = (cs < js + L_SLC) & (cs + L_CMP > js) & (np.arange(rows)[:, None] < nc) & (np.arange(cols)[None, :] < nslc)
    return jnp.asarray(ov.astype(np.float32), dtype=BF16)


def _nsa_prompt(z128h, zplh, zpl, kc, vc, B, S):
    n_cmp = S // S_CMP - L_CMP // S_CMP + 1
    n_slc = -(-S // L_SLC)
    assert n_cmp <= 128 and n_slc <= 128 and kc.shape[1] == 128
    ov = _overlap_np(n_cmp, n_slc, 128, 128)
    ov = jnp.concatenate([ov, ov], axis=0)
    ex = jnp.asarray((np.arange(128)[:, None] == np.arange(S)[None, :] // L_SLC).astype(np.float32), dtype=BF16)
    blk = lambda w, f: pl.BlockSpec((1, Q_BLK, w), f)
    full = lambda w, c: pl.BlockSpec((1, S, w), lambda b, i: (b, 0, c))
    const = lambda shape: pl.BlockSpec(shape, lambda b, i: (0, 0))
    cspec = pl.BlockSpec((1, 128, 256), lambda b, i: (b, 0, 0))
    return pl.pallas_call(
        _by_key_class(_nsa_prompt_body, S, n_cmp=n_cmp, n_slc=n_slc, n_sel=min(N_SEL_MAX, n_slc)),
        grid=(B, S // Q_BLK), name='nsa_prompt',
        in_specs=[blk(1024, lambda b, i: (b, i, 1)), cspec, cspec, full(256, 10), full(256, 2), full(256, 11), full(256, 3),
                  blk(128, lambda b, i: (b, i, 8)), const((256, 128)), const((128, S))],
        out_specs=blk(1024, lambda b, i: (b, i, 0)),
        out_shape=jax.ShapeDtypeStruct((B, S, H_B * DH), F32),
        compiler_params=_params('parallel', 'parallel'),
    )(z128h, kc, vc, z128h, zplh, z128h, zplh, zpl, ov, ex)


def _merge_u_kernel(*refs, precise):
    if precise:
        oa_ref, ob_ref, wah_ref, wal_ref, wbh_ref, wbl_ref, ga_ref, gb_ref, o_ref = refs
        a = _dot3w(oa_ref[...], wah_ref[...], wal_ref[...])
        b = _dot3w(ob_ref[...], wbh_ref[...], wbl_ref[...])
    else:
        oa_ref, ob_ref, wa_ref, wb_ref, ga_ref, gb_ref, o_ref = refs
        a, b = _dot(oa_ref[...], wa_ref[...]), _dot(ob_ref[...], wb_ref[...])
    o_ref[...] = (ga_ref[...] * a + gb_ref[...] * b).astype(o_ref.dtype)


def _merge(x, o_a, o_b, zg, wa, wb, wo, precise=False):
    M = x.shape[0]
    tm, tn = min(M, 512), 512
    nj = D_MODEL // tn
    wspec = lambda k: pl.BlockSpec((k, tn), lambda i, j: (0, j))
    ws = (wa + wb) if precise else (wa[0], wb[0])
    wspecs = [wspec(H_A * DH)] * (len(ws) // 2) + [wspec(H_B * DH)] * (len(ws) // 2)
    u = pl.pallas_call(
        functools.partial(_merge_u_kernel, precise=precise), grid=(M // tm, nj), name='merge_u',
        in_specs=[pl.BlockSpec((tm, H_A * DH), lambda i, j: (i, 0)), pl.BlockSpec((tm, H_B * DH), lambda i, j: (i, 0))] + wspecs
        + [pl.BlockSpec((tm, tn), lambda i, j: (i, j)), pl.BlockSpec((tm, tn), lambda i, j: (i, j + nj))],
        out_specs=pl.BlockSpec((tm, tn), lambda i, j: (i, j)),
        out_shape=jax.ShapeDtypeStruct((M, D_MODEL), F32 if precise else BF16),
        compiler_params=_params('parallel', 'parallel'),
    )(o_a, o_b, *ws, zg, zg)
    return _mm(u, wo if precise else wo[0], 512, 'res', extra=(x,), name='merge_out')


def _ffn_kernel(te_ref, nu_ref, tr_ref, h_ref, *rest, has_res, n_var, precise):
    nw = 6 if precise else 3
    w = rest[:nw]
    o_ref, acc_ref = rest[-2], rest[-1]
    m, f = pl.program_id(0), pl.program_id(1)
    tm = acc_ref.shape[0]
    valid = tr_ref[m]

    @pl.when(f == 0)
    def _():
        acc_ref[...] = jnp.zeros_like(acc_ref)

    def part(rows):
        def run():
            h = h_ref[:rows, :]
            if precise:
                a = _dot3w(h, w[0][0], w[1][0])
                b = _dot3w(h, w[2][0], w[3][0])
                acc_ref[:rows, :] += _dot3w(a * jax.nn.sigmoid(a) * b, w[4][0], w[5][0])
            else:
                a = jnp.dot(h, w[0][0], preferred_element_type=F32)
                b = jnp.dot(h, w[1][0], preferred_element_type=F32)
                acc_ref[:rows, :] += _dot(a * jax.nn.sigmoid(a) * b, w[2][0])
        return run

    step = tm // n_var
    for c in range(n_var):
        pl.when((valid > c * step) & (valid <= (c + 1) * step))(part((c + 1) * step))

    @pl.when(f == pl.num_programs(1) - 1)
    def _():
        o_ref[...] = acc_ref[...] + rest[nw][...] if has_res else acc_ref[...]


def _ffn(h, wg, wu, wd, tile_expert, n_used, tile_rows, res=None):
    precise = isinstance(wg, tuple)
    R, D = h.shape
    F = (wg[0] if precise else wg).shape[2]
    tm, tf = min(R, FFN_TM), 512
    nf = F // tf
    fe = lambda m, f, nu: jnp.where(m < nu[0], f, nf - 1)
    up_spec = pl.BlockSpec((1, D, tf), lambda m, f, te, nu, tr: (te[m], 0, fe(m, f, nu)))
    down_spec = pl.BlockSpec((1, tf, D), lambda m, f, te, nu, tr: (te[m], fe(m, f, nu), 0))
    rep = 2 if precise else 1
    in_specs = [pl.BlockSpec((tm, D), lambda m, f, te, nu, tr: (m, 0))] + [up_spec] * (2 * rep) + [down_spec] * rep
    args = [h] + (list(wg + wu + wd) if precise else [wg, wu, wd])
    if res is not None:
        in_specs.append(pl.BlockSpec((tm, D), lambda m, f, te, nu, tr: (m, 0)))
        args.append(res)
    return pl.pallas_call(
        functools.partial(_ffn_kernel, has_res=res is not None, n_var=4 if tm == FFN_TM else 1, precise=precise), name='ffn',
        grid_spec=pltpu.PrefetchScalarGridSpec(
            num_scalar_prefetch=3, grid=(R // tm, nf), in_specs=in_specs,
            out_specs=pl.BlockSpec((tm, D), lambda m, f, te, nu, tr: (m, 0)),
            scratch_shapes=[pltpu.VMEM((tm, D), F32)]),
        out_shape=jax.ShapeDtypeStruct((R, D), F32),
        compiler_params=_params('parallel', 'arbitrary'),
    )(tile_expert, n_used, tile_rows, *args)


def _dense_ffn(x, g, wg, wu, wd, precise=False):
    h = _rms(x, g, F32 if precise else BF16)
    tm = min(x.shape[0], FFN_TM)
    n_tiles = x.shape[0] // tm
    pick = (lambda w: (w[0][None], w[1][None])) if precise else (lambda w: w[0][None])
    return _ffn(h, pick(wg), pick(wu), pick(wd), jnp.zeros((n_tiles,), I32), jnp.full((1,), n_tiles, I32),
                jnp.full((n_tiles,), tm, I32), res=x)


def _router_kernel(x_ref, g_ref, wh_ref, wl_ref, h_ref, idx_ref, gate_ref):
    x = x_ref[...]
    h = x * lax.rsqrt(jnp.mean(x * x, axis=-1, keepdims=True) + EPS) * g_ref[...]
    h_ref[...] = h.astype(BF16)
    hh, hl = _split(h)
    d = lambda a, b: jnp.dot(a, b, preferred_element_type=F32)
    logits = d(hh, wh_ref[...]) + d(hh, wl_ref[...]) + d(hl, wh_ref[...])
    lane = lax.broadcasted_iota(I32, logits.shape, 1)
    lanef = lane.astype(F32)
    l1 = jnp.where(lane < N_EXP, logits, -jnp.inf)
    m1 = jnp.max(l1, axis=1, keepdims=True)
    i1 = jnp.min(jnp.where(l1 == m1, lanef, 128.0), axis=1, keepdims=True)
    l2 = jnp.where(lanef == i1, -jnp.inf, l1)
    m2 = jnp.max(l2, axis=1, keepdims=True)
    i2 = jnp.min(jnp.where(l2 == m2, lanef, 128.0), axis=1, keepdims=True)
    e = jnp.exp(m2 - m1)
    inv = 1.0 / (1.0 + e)
    idx_ref[...] = jnp.where(lane == 0, i1, i2).astype(I32)
    gate_ref[...] = jnp.where(lane == 0, inv, e * inv)


def _router(x, g, w_r):
    M, D = x.shape
    tm = min(M, 512)
    wpad = jnp.pad(w_r, ((0, 0), (0, 128 - N_EXP)))
    wh = wpad.astype(BF16)
    wl = (wpad - wh.astype(F32)).astype(BF16)
    row = lambda w, dt: (pl.BlockSpec((tm, w), lambda i: (i, 0)), jax.ShapeDtypeStruct((M, w), dt))
    outs = [row(D, BF16), row(128, I32), row(128, F32)]
    h, idx, gate = pl.pallas_call(
        _router_kernel, grid=(M // tm,), name='router',
        in_specs=[pl.BlockSpec((tm, D), lambda i: (i, 0)), pl.BlockSpec((1, D), lambda i: (0, 0)),
                  pl.BlockSpec((D, 128), lambda i: (0, 0)), pl.BlockSpec((D, 128), lambda i: (0, 0))],
        out_specs=[o[0] for o in outs], out_shape=[o[1] for o in outs],
        compiler_params=_params('parallel'),
    )(x, g.reshape(1, D), wh, wl)
    return h, idx[:, :2], gate[:, :2]


def _moe(xp, xs, g, w_r, wg, wu, wd):
    hp, ip, gp = _router(xp, g, w_r)
    hs, is_, gs = _router(xs, g, w_r)
    h = jnp.concatenate([hp, hs], axis=0)
    idx = jnp.concatenate([ip, is_], axis=0)
    gate = jnp.concatenate([gp, gs], axis=0)
    M = h.shape[0]
    tm = FFN_TM
    n_tiles = (2 * M + N_EXP * (tm - 1) + tm - 1) // tm
    e = jnp.concatenate([idx[:, 0], idx[:, 1]])
    onehot = (e[:, None] == jnp.arange(N_EXP, dtype=I32)[None, :]).astype(I32)
    rank = jnp.sum((jnp.cumsum(onehot, axis=0) - onehot) * onehot, axis=1)
    counts = jnp.sum(onehot, axis=0)
    tiles_per = (counts + tm - 1) // tm
    tile_end = jnp.cumsum(tiles_per)
    row_start = (tile_end - tiles_per) * tm
    pos = jnp.sum(onehot * row_start[None, :], axis=1) + rank
    src_tok = jnp.zeros((n_tiles * tm,), I32).at[pos].set(jnp.arange(2 * M, dtype=I32) % M)
    n_used = tile_end[-1].astype(I32)
    tid = jnp.arange(n_tiles, dtype=I32)
    t = jnp.minimum(tid, n_used - 1)
    tile_expert = jnp.sum((t[:, None] >= tile_end[None, :]).astype(I32), axis=1)
    rows_left = counts[tile_expert] - (t - (tile_end - tiles_per)[tile_expert]) * tm
    tile_rows = jnp.where(tid < n_used, jnp.clip(rows_left, 0, tm), 0).astype(I32)
    y = _ffn(jnp.take(h, src_tok, axis=0), wg, wu, wd, tile_expert, n_used.reshape(1), tile_rows)
    f = gate[:, 0:1] * jnp.take(y, pos[:M], axis=0) + gate[:, 1:2] * jnp.take(y, pos[M:], axis=0)
    Mp = xp.shape[0]
    return xp + f[:Mp], xs + f[Mp:]


def _page_specs(block, n_lead, layer, col=None):
    def spec(j):
        def imap(b, s, pt):
            page = pt[b * (PAST_LEN // PAGE_SIZE) + s * PAGES_PER_STEP + j]
            return (layer, page) + (0,) * n_lead
        return pl.BlockSpec(block, imap)
    return [spec(j) for j in range(PAGES_PER_STEP)]


def _idx_walk_kernel(pt_ref, q_ref, w_ref, *refs):
    pages, o_ref = refs[:-1], refs[-1]
    keys = jnp.concatenate([r[0, 0] for r in pages], axis=0)
    d = _dot3_nt(q_ref[0], keys) * (D_IDX ** -0.5)
    o_ref[0, 0] = jnp.sum(jnp.maximum(d, 0.0) * (w_ref[0] * (H_I ** -0.5)), axis=0, keepdims=True)


def _dsa_thresh_kernel(sc_ref, qi_ref, wi_ref, kin_ref, sel_ref, selnew_ref, *, k_sel):
    nb = sc_ref.shape[0]
    snew = []
    for b in range(nb):
        d = jnp.sum(qi_ref[b] * kin_ref[b:b + 1, :], axis=1, keepdims=True) * (D_IDX ** -0.5)
        snew.append(jnp.sum(jnp.maximum(d, 0.0) * (wi_ref[b] * (H_I ** -0.5)), axis=0, keepdims=True))
    snew = jnp.concatenate(snew, axis=0)
    key, knew = _order_key(sc_ref[...]), _order_key(snew)
    count_ge = lambda c: (jnp.sum(jnp.where(key >= c, 1.0, 0.0), axis=1, keepdims=True) + jnp.where(knew >= c, 1.0, 0.0))
    thr = _kth_largest_key(count_ge, nb, k_sel)
    sel_ref[...] = jnp.where((key >= thr) & (sc_ref[...] > NEG * 0.5), 1.0, 0.0)
    selnew_ref[...] = jnp.broadcast_to(jnp.where((knew >= thr) & (snew > NEG * 0.5), 1.0, 0.0), selnew_ref.shape)


def _dsa_walk_kernel(pt_ref, q_ref, sel_ref, selnew_ref, knew_ref, vnew_ref, *refs):
    n = PAGES_PER_STEP
    kpages, vpages = refs[:n], refs[n:2 * n]
    o_ref, m_ref, l_ref, acc_ref = refs[2 * n:]
    s_id = pl.program_id(1)
    row = lax.broadcasted_iota(I32, (H_A, 1), 0)
    first = row < G

    @pl.when(s_id == 0)
    def _():
        m_ref[...] = jnp.full_like(m_ref, NEG)
        l_ref[...] = jnp.zeros_like(l_ref)
        acc_ref[...] = jnp.zeros_like(acc_ref)

    q = q_ref[0]
    k = jnp.concatenate([r[0, 0] for r in kpages], axis=0)
    v = jnp.concatenate([r[0, 0] for r in vpages], axis=0)
    col = lax.broadcasted_iota(I32, (1, k.shape[0]), 1)
    mask = jnp.where((col & 1) == jnp.where(first, 0, 1), sel_ref[0, 0], 0.0) > 0.5
    s = jnp.where(mask, _dot3_nt(q, k) * SCALE, NEG)
    m_new = jnp.maximum(m_ref[...], jnp.max(s, axis=1, keepdims=True))
    p = jnp.where(mask, jnp.exp(s - m_new), 0.0)
    alpha = jnp.exp(m_ref[...] - m_new)
    l_ref[...] = alpha * l_ref[...] + jnp.sum(p, axis=1, keepdims=True)
    acc_ref[...] = alpha * acc_ref[...] + _dot3(p, v)
    m_ref[...] = m_new

    @pl.when(s_id == pl.num_programs(1) - 1)
    def _():
        kn = jnp.where(first, knew_ref[0][:, :DH], knew_ref[0][:, DH:])
        vn = jnp.where(first, vnew_ref[0][:, :DH], vnew_ref[0][:, DH:])
        valid = selnew_ref[0][:, :1] > 0.5
        sn = jnp.where(valid, jnp.sum(q * kn, axis=1, keepdims=True) * SCALE, NEG)
        m_fin = jnp.maximum(m_ref[...], sn)
        pn = jnp.where(valid, jnp.exp(sn - m_fin), 0.0)
        a2 = jnp.exp(m_ref[...] - m_fin)
        l_fin = a2 * l_ref[...] + pn
        o_ref[0] = (a2 * acc_ref[...] + pn * vn) * (1.0 / jnp.maximum(l_fin, 1e-30))


def _dsa_sample(l, z128, z64, zpl, cache_k, cache_v, cache_ki, pt_flat, DB):
    n_pages = PAST_LEN // PAGE_SIZE
    n_steps = n_pages // PAGES_PER_STEP
    span = PAGES_PER_STEP * PAGE_SIZE
    k_sel = min(TOPK_MAX, (PAST_LEN + 1) // 4)
    qi = z64[:DB, :H_I * D_IDX].reshape(DB, H_I, D_IDX)
    wi = zpl[:DB, 1024:1024 + H_I].reshape(DB, H_I, 1)
    kin = z64[:DB, 1024:1024 + D_IDX]
    scores = pl.pallas_call(
        _idx_walk_kernel, name='idx_walk',
        grid_spec=pltpu.PrefetchScalarGridSpec(
            num_scalar_prefetch=1, grid=(DB, n_steps),
            in_specs=[pl.BlockSpec((1, H_I, D_IDX), lambda b, s, pt: (b, 0, 0)), pl.BlockSpec((1, H_I, 1), lambda b, s, pt: (b, 0, 0))]
            + _page_specs((1, 1, PAGE_SIZE, D_IDX), 2, l),
            out_specs=pl.BlockSpec((1, 1, 1, span), lambda b, s, pt: (b, s, 0, 0))),
        out_shape=jax.ShapeDtypeStruct((DB, n_steps, 1, span), F32),
        compiler_params=_params('parallel', 'parallel'),
    )(pt_flat, qi, wi, *([cache_ki] * PAGES_PER_STEP))
    sel, selnew = pl.pallas_call(
        functools.partial(_dsa_thresh_kernel, k_sel=k_sel), name='dsa_thresh',
        out_shape=[jax.ShapeDtypeStruct((DB, PAST_LEN), F32), jax.ShapeDtypeStruct((DB, 128), F32)],
        compiler_params=pltpu.CompilerParams(vmem_limit_bytes=VMEM_LIMIT),
    )(scores.reshape(DB, PAST_LEN), qi, wi, kin)
    qa = z128[:DB, :H_A * DH].reshape(DB, H_A, DH)
    knew = jnp.broadcast_to(z128[:DB, None, 2048:2304], (DB, H_A, KVH_A * DH))
    vnew = jnp.broadcast_to(zpl[:DB, None, 0:256], (DB, H_A, KVH_A * DH))
    selnew = jnp.broadcast_to(selnew[:, None, :], (DB, H_A, 128))
    per_b = lambda shape: pl.BlockSpec(shape, lambda b, s, pt: (b, 0, 0))
    o = pl.pallas_call(
        _dsa_walk_kernel, name='dsa_walk',
        grid_spec=pltpu.PrefetchScalarGridSpec(
            num_scalar_prefetch=1, grid=(DB, n_steps),
            in_specs=[per_b((1, H_A, DH)), pl.BlockSpec((1, 1, 1, KVH_A * span), lambda b, s, pt: (b, s, 0, 0)),
                      per_b((1, H_A, 128)), per_b((1, H_A, KVH_A * DH)), per_b((1, H_A, KVH_A * DH))]
            + _page_specs((1, 1, KVH_A * PAGE_SIZE, DH), 2, l) * 2,
            out_specs=per_b((1, H_A, DH)),
            scratch_shapes=[pltpu.VMEM((H_A, 1), F32), pltpu.VMEM((H_A, 1), F32), pltpu.VMEM((H_A, DH), F32)]),
        out_shape=jax.ShapeDtypeStruct((DB, H_A, DH), F32),
        compiler_params=_params('parallel', 'arbitrary'),
    )(pt_flat, qa, jnp.repeat(sel, KVH_A, axis=1).reshape(DB, n_steps, 1, KVH_A * span), selnew, knew, vnew,
      *([cache_k] * PAGES_PER_STEP), *([cache_v] * PAGES_PER_STEP))
    return o.reshape(DB, H_A * DH)


def _compress_sample(l, cache_k, cache_v, cw, pt_flat, DB):
    pek, w1k, w2k, pev, w1v, w2v = cw
    n_pages = PAST_LEN // PAGE_SIZE
    n_steps = n_pages // PAGES_PER_STEP
    cps = PAGE_SIZE // S_CMP
    nch_step = PAGES_PER_STEP * cps
    const = lambda b, s, pt: (0, 0)
    out_spec = pl.BlockSpec((1, nch_step, KVH_B * DH), lambda b, s, pt: (b, s, 0))
    page_specs = _page_specs((1, 1, KVH_B * PAGE_SIZE, DH), 2, l)
    parts = pl.pallas_call(
        functools.partial(_cmp_partial_kernel, n_src=PAGES_PER_STEP, paged=True, precise=True), name='cmp_partial_sample',
        grid_spec=pltpu.PrefetchScalarGridSpec(
            num_scalar_prefetch=1, grid=(DB, n_steps),
            in_specs=page_specs + page_specs + _cmp_weight_specs(const),
            out_specs=[out_spec] * 4),
        out_shape=[jax.ShapeDtypeStruct((DB, PAST_LEN // S_CMP, KVH_B * DH), F32)] * 4,
        compiler_params=_params('parallel', 'parallel'),
    )(pt_flat, *([cache_k] * PAGES_PER_STEP), *([cache_v] * PAGES_PER_STEP), pek, *_hi_lo(w1k), pev, *_hi_lo(w1v))
    return _cmp_finish(parts, w2k, w2v, precise=True)


def _cmp_select_sample_kernel(q_ref, kc_ref, vc_ref, ov_ref, o_ref, ids_ref, *, n_cmp, n_slc, n_sel):
    n = kc_ref.shape[1]
    q = q_ref[0]
    row = lax.broadcasted_iota(I32, (H_B, 1), 0)
    first = row < G
    kc = kc_ref[0]
    vc = vc_ref[0]
    s = jnp.where(first, _dot3_nt(q, kc[:, :DH]), _dot3_nt(q, kc[:, DH:])) * SCALE
    col = lax.broadcasted_iota(I32, (1, n), 1)
    qpos = PAST_LEN
    pr = _mprobs(s, (col * S_CMP + (L_CMP - 1) <= qpos) & (col < n_cmp))
    o_ref[0] = jnp.where(first, _dot3(pr, vc[:, :DH]), _dot3(pr, vc[:, DH:]))
    psum = jnp.concatenate([jnp.sum(jnp.where((row >= kv * G) & (row < (kv + 1) * G), pr, 0.0), axis=0, keepdims=True)
                            for kv in range(KVH_B)] + [jnp.zeros((8 - KVH_B, n), F32)], axis=0)
    ph, plo = _split(psum)
    imp = jnp.dot(ph, ov_ref[...], preferred_element_type=F32) + jnp.dot(plo, ov_ref[...], preferred_element_type=F32)
    j = lax.broadcasted_iota(I32, imp.shape, 1)
    jf = j.astype(F32)
    jt = qpos // L_SLC
    forced = (j == 0) | (j == jt) | (j == jt - 1)
    sc = jnp.where((j > jt) | (j >= n_slc), NEG, jnp.where(forced, FORCED, imp))
    lane = lax.broadcasted_iota(I32, (8, 128), 1)
    ids = jnp.zeros((8, 128), F32)
    for r in range(n_sel):
        m = jnp.max(sc, axis=1, keepdims=True)
        pick = jnp.min(jnp.where(sc == m, jf, 1e9), axis=1, keepdims=True)
        ids = jnp.where(lane == r, pick, ids)
        sc = jnp.where(jf == pick, -jnp.inf, sc)
    ids_ref[0] = ids.astype(I32)


def _slc_sample_kernel(ids_ref, pt_ref, q_ref, knew_ref, vnew_ref, *refs, n_sel, n_past):
    kblks, vblks, o_ref = refs[:n_sel], refs[n_sel:2 * n_sel], refs[2 * n_sel]
    b, kv = pl.program_id(0), pl.program_id(1)
    base = (b * KVH_B + kv) * n_sel
    nrow = KVH_B * L_SLC
    row = lax.broadcasted_iota(I32, (nrow, 1), 0)
    lane = lax.broadcasted_iota(I32, (1, n_sel * nrow), 1)
    tpos = (lane % nrow) // KVH_B
    ks, vs = [], []
    for r in range(n_sel):
        bid = ids_ref[base + r]
        ks.append(jnp.where(bid < n_past, kblks[r][0, 0, 0], jnp.where(row == kv, knew_ref[0, 0], 0.0)))
        vs.append(jnp.where(bid < n_past, vblks[r][0, 0, 0], jnp.where(row == kv, vnew_ref[0, 0], 0.0)))
        tpos = tpos + jnp.where(lane // nrow == r, bid * L_SLC, 0)
    mask = jnp.where(lane % KVH_B == kv, tpos, PAST_LEN + 1) <= PAST_LEN
    s = _dot3_nt(q_ref[0, 0], jnp.concatenate(ks, axis=0)) * SCALE
    o_ref[0, 0] = _dot3(_mprobs(s, mask), jnp.concatenate(vs, axis=0))


def _win_sample_kernel(q_ref, kb_ref, vb_ref, kn_ref, vn_ref, o_ref, ko_ref, vo_ref):
    W = kb_ref.shape[2]
    q = q_ref[0]
    rowq = lax.broadcasted_iota(I32, (H_B, 1), 0)
    first = rowq < G
    kb, vb = kb_ref[0, 0], vb_ref[0, 0]
    kn, vn = kn_ref[0], vn_ref[0]
    s = jnp.where(first, _dot3_nt(q, kb[:, :DH]), _dot3_nt(q, kb[:, DH:])) * SCALE
    col = lax.broadcasted_iota(I32, (1, W), 1)
    mask = (PAST_LEN - W + col) > (PAST_LEN - WINDOW)
    knh = jnp.where(first, kn[:, :DH], kn[:, DH:])
    vnh = jnp.where(first, vn[:, :DH], vn[:, DH:])
    sn = jnp.sum(q * knh, axis=1, keepdims=True) * SCALE
    s = jnp.where(mask, s, NEG)
    m = jnp.maximum(jnp.max(s, axis=1, keepdims=True), sn)
    p = jnp.where(mask, jnp.exp(s - m), 0.0)
    pn = jnp.exp(sn - m)
    inv = 1.0 / jnp.maximum(jnp.sum(p, axis=1, keepdims=True) + pn, 1e-30)
    pn_ = p * inv
    pv = jnp.where(first, _dot3(pn_, vb[:, :DH]), _dot3(pn_, vb[:, DH:]))
    o_ref[0] = pv + (pn * inv) * vnh
    roww = lax.broadcasted_iota(I32, (W, 1), 0)
    ko_ref[0] = jnp.where(roww == W - 1, kn[:1, :], pltpu.roll(kb, W - 1, 0))
    vo_ref[0] = jnp.where(roww == W - 1, vn[:1, :], pltpu.roll(vb, W - 1, 0))


def _nsa_sample(l, z128, zpl, cache_ck, cache_cv, cache_sk, cache_sv, buf_k, buf_v, cw, pt_flat, DB):
    kc, vc = _compress_sample(l, cache_ck, cache_cv, cw, pt_flat, DB)
    n = kc.shape[1]
    L = PAST_LEN + 1
    n_cmp = L // S_CMP - L_CMP // S_CMP + 1
    n_slc = -(-L // L_SLC)
    n_sel = min(N_SEL_MAX, n_slc)
    n_past = PAST_LEN // L_SLC
    cols = -(-n_slc // 128) * 128
    qb = z128[:DB, 1024:2048].reshape(DB, H_B, DH)
    per_b = lambda shape: pl.BlockSpec(shape, lambda b: (b,) + (0,) * (len(shape) - 1))
    o_cmp, ids = pl.pallas_call(
        functools.partial(_cmp_select_sample_kernel, n_cmp=n_cmp, n_slc=n_slc, n_sel=n_sel), grid=(DB,), name='cmp_select_sample',
        in_specs=[per_b((1, H_B, DH)), per_b((1, n, 256)), per_b((1, n, 256)), pl.BlockSpec((n, cols), lambda b: (0, 0))],
        out_specs=[per_b((1, H_B, DH)), per_b((1, 8, 128))],
        out_shape=[jax.ShapeDtypeStruct((DB, H_B, DH), F32), jax.ShapeDtypeStruct((DB, 8, 128), I32)],
        compiler_params=_params('parallel'),
    )(qb, kc, vc, _overlap_np(n_cmp, n_slc, n, cols))
    ids_flat = ids[:, :KVH_B, :n_sel].reshape(-1)
    q4 = jnp.pad(qb.reshape(DB, KVH_B, G, DH), ((0, 0), (0, 0), (0, 8 - G), (0, 0)))
    knew = z128[:DB, 2560:2816].reshape(DB, KVH_B, 1, DH)
    vnew = zpl[:DB, 512:768].reshape(DB, KVH_B, 1, DH)
    bpp = PAGE_SIZE // L_SLC
    n_pages = PAST_LEN // PAGE_SIZE

    nrow = KVH_B * L_SLC

    def blk_spec(r):
        def imap(b, kv, ids_s, pt):
            bid = jnp.minimum(ids_s[(b * KVH_B + kv) * n_sel + r], n_past - 1)
            return (l, pt[b * n_pages + bid // bpp], bid % bpp, 0, 0)
        return pl.BlockSpec((1, 1, 1, nrow, DH), imap)

    per_bk = lambda shape: pl.BlockSpec(shape, lambda b, kv, ids_s, pt: (b, kv, 0, 0))
    view = lambda c: c.reshape(c.shape[0], c.shape[1], bpp, nrow, DH)
    blk_specs = [blk_spec(r) for r in range(n_sel)]
    o_slc = pl.pallas_call(
        functools.partial(_slc_sample_kernel, n_sel=n_sel, n_past=n_past), name='slc_sample',
        grid_spec=pltpu.PrefetchScalarGridSpec(
            num_scalar_prefetch=2, grid=(DB, KVH_B),
            in_specs=[per_bk((1, 1, 8, DH)), per_bk((1, 1, 1, DH)), per_bk((1, 1, 1, DH))] + blk_specs + blk_specs,
            out_specs=per_bk((1, 1, 8, DH))),
        out_shape=jax.ShapeDtypeStruct((DB, KVH_B, 8, DH), F32),
        compiler_params=_params('parallel', 'parallel'),
    )(ids_flat, pt_flat, q4, knew, vnew, *([view(cache_sk)] * n_sel), *([view(cache_sv)] * n_sel))
    W = buf_k.shape[2]
    kn8 = jnp.broadcast_to(z128[:DB, None, 2816:3072], (DB, 8, 256))
    vn8 = jnp.broadcast_to(zpl[:DB, None, 768:1024], (DB, 8, 256))
    bufspec = pl.BlockSpec((1, 1, W, 256), lambda b: (l, b, 0, 0))
    o_win, nk, nv = pl.pallas_call(
        _win_sample_kernel, grid=(DB,), name='win_sample',
        in_specs=[per_b((1, H_B, DH)), bufspec, bufspec, per_b((1, 8, 256)), per_b((1, 8, 256))],
        out_specs=[per_b((1, H_B, DH)), per_b((1, W, 256)), per_b((1, W, 256))],
        out_shape=[jax.ShapeDtypeStruct((DB, H_B, DH), F32)] + [jax.ShapeDtypeStruct((DB, W, 256), F32)] * 2,
        compiler_params=_params('parallel'),
    )(qb, buf_k.reshape(buf_k.shape[0], DB, W, 256), buf_v.reshape(buf_v.shape[0], DB, W, 256), kn8, vn8)
    gb = jax.nn.sigmoid(zpl[:DB, 1024 + GATE_OFF:1024 + GATE_OFF + 3 * H_B]).reshape(DB, H_B, 3)
    o_b = gb[..., 0:1] * o_cmp + gb[..., 1:2] * o_slc[:, :, :G].reshape(DB, H_B, DH) + gb[..., 2:3] * o_win
    return o_b.reshape(DB, H_B * DH), nk, nv


def kernel(x_prompt, x_sample, cache_dsa_k, cache_dsa_v, cache_idx_k, cache_cmp_k, cache_cmp_v, cache_slc_k, cache_slc_v, state_win_k, state_win_v, page_table, norm_mix, w_in, cmp_pe_k, cmp_w1_k, cmp_w2_k, cmp_pe_v, cmp_w1_v, cmp_w2_v, w_up_a, w_up_b, w_o, norm_ffn, dense_w_gate, dense_w_up, dense_w_down, moe_w_router, moe_w_gate, moe_w_up, moe_w_down, norm_final):
    B, S, D = x_prompt.shape
    DB, T, _ = x_sample.shape
    depth = w_in.shape[0]
    assert T == 1 and D == D_MODEL and page_table.shape == (DB, PAST_LEN // PAGE_SIZE)
    wp = min(WINDOW, S)
    xp = x_prompt.reshape(B * S, D)
    xs = jnp.pad(x_sample.reshape(DB, D), ((0, SAMPLE_ROWS - DB), (0, 0)))
    pt_flat = page_table.reshape(-1).astype(I32)
    pos_p = jnp.tile(jnp.arange(S), B)
    pos_s = jnp.full((SAMPLE_ROWS,), PAST_LEN)
    tabs_p = (_rope_tables(pos_p, ROT_DIM, DH), _rope_tables(pos_p, ROT_IDX, D_IDX))
    tabs_s = (_rope_tables(pos_s, ROT_DIM, DH), _rope_tables(pos_s, ROT_IDX, D_IDX))
    rows_view = lambda c: c.reshape(c.shape[0], c.shape[1], -1, c.shape[-1])
    c_dsa_k, c_dsa_v, c_idx, c_cmp_k, c_cmp_v, c_slc_k, c_slc_v = map(
        rows_view, (cache_dsa_k, cache_dsa_v, cache_idx_k, cache_cmp_k, cache_cmp_v, cache_slc_k, cache_slc_v))
    newp, news = [], []
    for l in range(depth):
        cw = (cmp_pe_k[l], cmp_w1_k[l], cmp_w2_k[l], cmp_pe_v[l], cmp_w1_v[l], cmp_w2_v[l])
        wsplit = _split_w_in(w_in[l])
        wa, wb, wo = _hi_lo(w_up_a[l]), _hi_lo(w_up_b[l]), _hi_lo(w_o[l])

        z128, z64, zpl, zg = _project(_rms(xp, norm_mix[l], BF16), wsplit, *tabs_p)
        (z128b, z128h), (z64b, z64h, z64l), (zplb, zplh) = (tuple(t.reshape(B, S, -1) for t in z) for z in (z128, z64, zpl))
        o_a = _dsa_prompt(z128h, z64h, z64l, zplh, zplb, B, S)
        kc, vc = _compress_prompt(z128b, zplb, cw, B, S)
        o_b = _nsa_prompt(z128h, zplh, zplb, kc, vc, B, S)
        xp = _merge(xp, o_a.reshape(B * S, -1), o_b.reshape(B * S, -1), zg, wa, wb, wo)
        kv4 = lambda z, c: z[:, :, c:c + 256].reshape(B, S, 2, DH)
        newp.append((kv4(z128b, 2048), kv4(zplb, 0), z64b[:, :, 1024:1024 + D_IDX], kv4(z128b, 2304), kv4(zplb, 256),
                     kv4(z128b, 2560), kv4(zplb, 512), kv4(z128b, 2816)[:, S - wp:], kv4(zplb, 768)[:, S - wp:]))

        y128, y64, ypl, yg = _project(_rms(xs, norm_mix[l], F32), wsplit, *tabs_s, precise=True)
        o_a = _dsa_sample(l, y128, y64, ypl, c_dsa_k, c_dsa_v, c_idx, pt_flat, DB)
        o_b, win_k, win_v = _nsa_sample(l, y128, ypl, c_cmp_k, c_cmp_v, c_slc_k, c_slc_v, state_win_k, state_win_v, cw, pt_flat, DB)
        pad = lambda o: jnp.pad(o, ((0, SAMPLE_ROWS - DB), (0, 0)))
        xs = _merge(xs, pad(o_a), pad(o_b), yg, wa, wb, wo, precise=True)
        s4 = lambda z, c: z[:DB, c:c + 256].reshape(DB, 1, 2, DH)
        news.append((s4(y128, 2048), s4(ypl, 0), y64[:DB, 1024:1024 + D_IDX].reshape(DB, 1, D_IDX), s4(y128, 2304), s4(ypl, 256),
                     s4(y128, 2560), s4(ypl, 512), win_k.reshape(DB, -1, 2, DH), win_v.reshape(DB, -1, 2, DH)))

        i = l // 2
        if l % 2 == 0:
            ws = (_hi_lo(dense_w_gate[i]), _hi_lo(dense_w_up[i]), _hi_lo(dense_w_down[i]))
            xp = _dense_ffn(xp, norm_ffn[l], *ws)
            xs = _dense_ffn(xs, norm_ffn[l], *ws, precise=True)
        else:
            xp, xs = _moe(xp, xs, norm_ffn[l], moe_w_router[i], moe_w_gate[i].astype(BF16), moe_w_up[i].astype(BF16),
                          moe_w_down[i].astype(BF16))

    y_prompt = _rms(xp, norm_final, F32).reshape(B, S, D)
    y_sample = _rms(xs, norm_final, F32)[:DB].reshape(DB, 1, D)
    stack = lambda items: [jnp.stack([it[n] for it in items], axis=0) for n in range(9)]
    return (y_prompt, y_sample, *stack(newp), *stack(news))
```

```python
import functools

import numpy as np
import jax
import jax.numpy as jnp
from jax import lax
from jax.experimental import pallas as pl
from jax.experimental.pallas import tpu as pltpu

F32 = jnp.float32
BF16 = jnp.bfloat16
I32 = jnp.int32

D_MODEL = 2048
PAST_LEN = 16384
PAGE_SIZE = 128
DH = 128
ROT_DIM = DH // 4
ROPE_THETA = 500000.0
H_A = 8
KVH_A = 2
H_I = 16
D_IDX = 64
ROT_IDX = D_IDX // 4
TOPK_MAX = 256
H_B = 8
KVH_B = 2
G = 4
L_CMP = 32
S_CMP = 16
L_SLC = 64
N_SEL_MAX = 16
WINDOW = 512
N_EXP = 8
Q_BLK = 128
FFN_TM = 512
KEY_CLS = 512
EPS = 1e-6
NEG = -1e30
FORCED = 1e6
INT_MIN = -2 ** 31
SCALE = DH ** -0.5
PAGES_PER_STEP = 16
SAMPLE_ROWS = 16
VMEM_LIMIT = 48 * 1024 * 1024

_IN_SPLITS = (
    ('q_a', H_A * DH), ('k_a', KVH_A * DH), ('v_a', KVH_A * DH),
    ('q_i', H_I * D_IDX), ('k_i', D_IDX), ('w_i', H_I),
    ('q_b', H_B * DH),
    ('k_cmp', KVH_B * DH), ('v_cmp', KVH_B * DH),
    ('k_slc', KVH_B * DH), ('v_slc', KVH_B * DH),
    ('k_win', KVH_B * DH), ('v_win', KVH_B * DH),
    ('g_b', 3 * H_B), ('g_mix', 2 * D_MODEL),
)
_GROUP_ROPE128 = ('q_a', 'q_b', 'k_a', 'k_cmp', 'k_slc', 'k_win')
_GROUP_ROPE64 = ('q_i', 'k_i')
_GROUP_PLAIN = ('v_a', 'v_cmp', 'v_slc', 'v_win', 'w_i', 'g_b')
W_R64 = 1280
W_PLAIN = 1280
GATE_OFF = H_I


def _params(*sem):
    return pltpu.CompilerParams(dimension_semantics=sem, vmem_limit_bytes=VMEM_LIMIT)


def _dot(a, b):
    return jnp.dot(a.astype(BF16), b.astype(BF16), preferred_element_type=F32)


def _dg_nt(a, b):
    return lax.dot_general(a, b, (((1,), (1,)), ((), ())), preferred_element_type=F32)


def _dot_nt(a, b):
    return _dg_nt(a.astype(BF16), b.astype(BF16))


def _split(a):
    hi = a.astype(BF16)
    lo = (a - hi.astype(F32)).astype(BF16)
    return hi, lo


def _dot3_nt(a, b):
    ah, al = _split(a)
    bh, bl = _split(b)
    return _dg_nt(jnp.concatenate([ah, ah, al], axis=1), jnp.concatenate([bh, bl, bh], axis=1))


def _dot3(a, b):
    ah, al = _split(a)
    bh, bl = _split(b)
    return jnp.dot(jnp.concatenate([ah, ah, al], axis=1), jnp.concatenate([bh, bl, bh], axis=0), preferred_element_type=F32)


def _with_bf16(w):
    return w, w.astype(BF16)


def _mprobs(s, mask):
    s = jnp.where(mask, s, NEG)
    m = jnp.max(s, axis=-1, keepdims=True)
    p = jnp.where(mask, jnp.exp(s - m), 0.0)
    return p * (1.0 / jnp.maximum(jnp.sum(p, axis=-1, keepdims=True), 1e-30))


def _order_key(x):
    b = lax.bitcast_convert_type(x + 0.0, I32)
    return jnp.where(b < 0, b ^ jnp.int32(0x7FFFFFFF), b)


def _kth_largest_key(count_ge, rows, k, n_bits=32):
    def body(it, lo):
        cand = lo + jnp.left_shift(jnp.int32(1), 31 - it)
        return jnp.where(count_ge(cand) >= k, cand, lo)
    return lax.fori_loop(0, n_bits, body, jnp.full((rows, 1), INT_MIN, I32))


def _rms_kernel(x_ref, g_ref, o_ref):
    x = x_ref[...]
    y = x * lax.rsqrt(jnp.mean(x * x, axis=-1, keepdims=True) + EPS)
    o_ref[...] = (y * g_ref[...]).astype(o_ref.dtype)


def _rms(x, g, out_dtype):
    M, D = x.shape
    tm = min(M, 512)
    return pl.pallas_call(
        _rms_kernel, grid=(M // tm,), name='rms',
        in_specs=[pl.BlockSpec((tm, D), lambda i: (i, 0)), pl.BlockSpec((1, D), lambda i: (0, 0))],
        out_specs=pl.BlockSpec((tm, D), lambda i: (i, 0)),
        out_shape=jax.ShapeDtypeStruct((M, D), out_dtype),
        compiler_params=_params('parallel'),
    )(x, g.reshape(1, D))


def _mm_kernel(*refs, mode, shift, out_kinds, precise):
    if precise:
        acc = _dot3(refs[0][...], refs[1][...])
    else:
        acc = jnp.dot(refs[0][...], refs[1][...], preferred_element_type=F32)
    if mode == 'rope':
        c_ref, s1_ref, s2_ref = refs[2:5]
        tn = acc.shape[1]
        rep = tn // 128
        til = lambda r: jnp.concatenate([r[...]] * rep, axis=1)
        acc = (acc * til(c_ref) + pltpu.roll(acc, tn - shift, 1) * til(s1_ref)
               + pltpu.roll(acc, shift, 1) * til(s2_ref))
    elif mode == 'sigmoid':
        acc = jax.nn.sigmoid(acc)
    elif mode == 'res':
        acc = acc + refs[2][...]
    for kind, o_ref in zip(out_kinds, refs[len(refs) - len(out_kinds):]):
        if kind == 'lo':
            o_ref[...] = (acc - acc.astype(BF16).astype(F32)).astype(BF16)
        else:
            o_ref[...] = acc.astype(o_ref.dtype)


_KIND_DTYPE = {'f32': F32, 'bf16': BF16, 'lo': BF16}


def _mm(a, b, tn, mode='none', shift=0, extra=(), out_kinds=('f32',), precise=False, name='mm'):
    out_dtypes = tuple(_KIND_DTYPE[k] for k in out_kinds)
    M, K = a.shape
    N = b.shape[1]
    tm = min(M, 1024)
    in_specs = [pl.BlockSpec((tm, K), lambda i, j: (i, 0)), pl.BlockSpec((K, tn), lambda i, j: (0, j))]
    if mode == 'rope':
        in_specs += [pl.BlockSpec((tm, 128), lambda i, j: (i, 0))] * 3
    elif mode == 'res':
        in_specs += [pl.BlockSpec((tm, tn), lambda i, j: (i, j))]
    outs = pl.pallas_call(
        functools.partial(_mm_kernel, mode=mode, shift=shift, out_kinds=out_kinds, precise=precise),
        grid=(M // tm, N // tn), name=name,
        in_specs=in_specs, out_specs=[pl.BlockSpec((tm, tn), lambda i, j: (i, j))] * len(out_dtypes),
        out_shape=[jax.ShapeDtypeStruct((M, N), dt) for dt in out_dtypes],
        compiler_params=_params('parallel', 'parallel'),
    )(a, b, *extra)
    return outs[0] if len(out_dtypes) == 1 else tuple(outs)


def _rope_tables(pos, rot, period):
    half = rot // 2
    inv = jnp.power(jnp.float32(ROPE_THETA), -jnp.arange(half, dtype=F32) / half)
    ang = pos.astype(F32)[:, None] * inv[None, :]
    cos, sin = jnp.cos(ang), jnp.sin(ang)
    T = pos.shape[0]
    z = jnp.zeros((T, period - rot), F32)
    zh = jnp.zeros((T, half), F32)
    c = jnp.concatenate([cos, cos, jnp.ones((T, period - rot), F32)], axis=1)
    s1 = jnp.concatenate([-sin, zh, z], axis=1)
    s2 = jnp.concatenate([zh, sin, z], axis=1)
    rep = 128 // period
    return tuple(jnp.tile(t, (1, rep)) for t in (c, s1, s2))


def _split_w_in(w):
    parts, off = {}, 0
    for name, width in _IN_SPLITS:
        parts[name] = w[:, off:off + width]
        off += width
    cat = lambda names, total: jnp.pad(
        jnp.concatenate([parts[n] for n in names], axis=1),
        ((0, 0), (0, total - sum(parts[n].shape[1] for n in names))))
    zeros_ki = jnp.zeros_like(parts['k_i'])
    w64 = jnp.concatenate([parts['q_i'], parts['k_i'], zeros_ki, zeros_ki, parts['k_i']], axis=1)
    assert w64.shape[1] == W_R64
    return tuple(_with_bf16(g) for g in (cat(_GROUP_ROPE128, 3072), w64, cat(_GROUP_PLAIN, W_PLAIN), parts['g_mix']))


def _project(h, wsplit, tabs128, tabs64, precise=False):
    w128, w64, wpl, wg = (w[0] if precise else w[1] for w in wsplit)
    both = ('f32',) if precise else ('f32', 'bf16')
    z128 = _mm(h, w128, 512, 'rope', ROT_DIM // 2, tabs128, out_kinds=both, precise=precise, name='proj_rope128')
    z64 = _mm(h, w64, 640, 'rope', ROT_IDX // 2, tabs64, out_kinds=both + (() if precise else ('lo',)), precise=precise,
              name='proj_rope64')
    zpl = _mm(h, wpl, 640, out_kinds=both, precise=precise, name='proj_plain')
    zg = _mm(h, wg, 512, 'sigmoid', precise=precise, name='proj_gmix')
    return z128, z64, zpl, zg


def _stack_heads(q_ref, kv):
    return jnp.concatenate([q_ref[0, :, (kv * G + g) * DH:(kv * G + g + 1) * DH] for g in range(G)], axis=0).astype(BF16)


def _masked_attend(q, k, v, mask):
    s = _dg_nt(q, k) * SCALE
    p = jnp.concatenate([_mprobs(s[g * Q_BLK:(g + 1) * Q_BLK], mask).astype(BF16) for g in range(G)], axis=0)
    return jnp.dot(p, v, preferred_element_type=F32)


def _dsa_prompt_body(qa_ref, ka_ref, va_ref, qih_ref, qil_ref, kih_ref, kil_ref, ki2h_ref, ki2l_ref, wi_ref, o_ref,
                     *, k_sel, ext):
    i = pl.program_id(1)
    w = wi_ref[0][:, :H_I] * (H_I ** -0.5)
    npair = H_I * D_IDX // 128
    pairs = lambda r: jnp.concatenate([r[0, :, j * 128:(j + 1) * 128] for j in range(npair)], axis=0)
    qh = pairs(qih_ref)
    q3 = jnp.concatenate([qh, qh, pairs(qil_ref)], axis=1)
    parts = []
    for c0 in range(0, ext, KEY_CLS):
        rows = lambda r1, r2: jnp.concatenate([r1[0, c0:c0 + KEY_CLS, :], r2[0, c0:c0 + KEY_CLS, :]], axis=0)
        kh = rows(kih_ref, ki2h_ref)
        d = _dg_nt(q3, jnp.concatenate([kh, rows(kil_ref, ki2l_ref), kh], axis=1)) * (D_IDX ** -0.5)
        sc = None
        for j in range(npair):
            dj = d[j * Q_BLK:(j + 1) * Q_BLK]
            t = (jnp.maximum(dj[:, :KEY_CLS], 0.0) * w[:, 2 * j:2 * j + 1]
                 + jnp.maximum(dj[:, KEY_CLS:], 0.0) * w[:, 2 * j + 1:2 * j + 2])
            sc = t if sc is None else sc + t
        parts.append(sc)
    score = jnp.concatenate(parts, axis=1)
    qpos = i * Q_BLK + lax.broadcasted_iota(I32, (Q_BLK, 1), 0)
    kpos = lax.broadcasted_iota(I32, (1, ext), 1)
    masked = jnp.where(kpos <= qpos, score, NEG)
    key = jnp.where(masked > NEG * 0.5, _order_key(masked), INT_MIN)
    count_ge = lambda c: jnp.sum(jnp.where(key >= c, 1.0, 0.0), axis=1, keepdims=True)
    n_it = jnp.where((i + 1) * Q_BLK > k_sel, 32, 0)
    thr = _kth_largest_key(count_ge, Q_BLK, k_sel, n_it)
    sel = key >= jnp.maximum(thr, INT_MIN + 1)
    for kv in range(KVH_A):
        sl = slice(kv * DH, (kv + 1) * DH)
        o = _masked_attend(_stack_heads(qa_ref, kv), ka_ref[0, :ext, sl], va_ref[0, :ext, sl], sel)
        for g in range(G):
            o_ref[0, :, (kv * G + g) * DH:(kv * G + g + 1) * DH] = o[g * Q_BLK:(g + 1) * Q_BLK]


def _by_key_class(body, S, **kw):
    def kern(*refs):
        i = pl.program_id(1)
        for c in range(S // KEY_CLS):
            pl.when(i // (KEY_CLS // Q_BLK) == c)(functools.partial(body, *refs, ext=(c + 1) * KEY_CLS, **kw))
    return kern


def _dsa_prompt(z128h, z64h, z64l, zplh, zpl, B, S):
    k_sel = min(TOPK_MAX, S // 4)
    blk = lambda w, f: pl.BlockSpec((1, Q_BLK, w), f)
    full = lambda w, c: pl.BlockSpec((1, S, w), lambda b, i: (b, 0, c))
    return pl.pallas_call(
        _by_key_class(_dsa_prompt_body, S, k_sel=k_sel), grid=(B, S // Q_BLK), name='dsa_prompt',
        in_specs=[blk(1024, lambda b, i: (b, i, 0)), full(256, 8), full(256, 0),
                  blk(1024, lambda b, i: (b, i, 0)), blk(1024, lambda b, i: (b, i, 0)),
                  full(128, 8), full(128, 8), full(128, 9), full(128, 9), blk(128, lambda b, i: (b, i, 8))],
        out_specs=blk(1024, lambda b, i: (b, i, 0)),
        out_shape=jax.ShapeDtypeStruct((B, S, H_A * DH), F32),
        compiler_params=_params('parallel', 'parallel'),
    )(z128h, z128h, zplh, z64h, z64l, z64h, z64l, z64h, z64l, zpl)


def _cmp_partial_kernel(*refs, n_src, paged, precise):
    refs = refs[len(refs) - (2 * n_src + 8):]
    rw = KVH_B * DH

    def mm(x, w_ref, p):
        if not precise:
            return _dot(x, w_ref[3 * p * DH:(3 * p + 1) * DH, :])
        xh, xl = _split(x)
        return jnp.dot(jnp.concatenate([xh, xl, xh], axis=1), w_ref[3 * p * DH:3 * (p + 1) * DH, :], preferred_element_type=F32)

    def rows_of(r, p, kvh):
        if paged:
            return r[0, 0, pl.ds(KVH_B * p + kvh, PAGE_SIZE // S_CMP, stride=KVH_B * S_CMP), :]
        return r[0, :, p * rw + kvh * DH:p * rw + (kvh + 1) * DH]
    ksrc, vsrc = refs[:n_src], refs[n_src:2 * n_src]
    pek, w1k, pev, w1v, ak, bk, av, bv = refs[2 * n_src:]
    half = L_CMP // 2
    for srcs, pe_ref, w_ref, a_ref, b_ref in ((ksrc, pek, w1k, ak, bk), (vsrc, pev, w1v, av, bv)):
        for kvh in range(KVH_B):
            acc_a = acc_b = None
            for p in range(half):
                rows = jnp.concatenate([rows_of(r, p, kvh) for r in srcs], axis=0)
                da = mm(rows + pe_ref[p:p + 1, :], w_ref, p)
                db = mm(rows + pe_ref[half + p:half + p + 1, :], w_ref, half + p)
                acc_a = da if acc_a is None else acc_a + da
                acc_b = db if acc_b is None else acc_b + db
            a_ref[0, :, kvh * DH:(kvh + 1) * DH] = acc_a
            b_ref[0, :, kvh * DH:(kvh + 1) * DH] = acc_b


def _cmp_finish_kernel(ak_ref, bk_ref, av_ref, bv_ref, w2k_ref, w2v_ref, kc_ref, vc_ref, *, precise):
    n = ak_ref.shape[1]
    mm = _dot3 if precise else _dot
    for a_ref, b_ref, w2_ref, o_ref in ((ak_ref, bk_ref, w2k_ref, kc_ref), (av_ref, bv_ref, w2v_ref, vc_ref)):
        for kvh in range(KVH_B):
            sl = slice(kvh * DH, (kvh + 1) * DH)
            h = a_ref[0, :, sl] + pltpu.roll(b_ref[0, :, sl], n - 1, 0)
            o_ref[0, :, sl] = mm(jax.nn.gelu(h), w2_ref[...])


def _cmp_finish(parts, w2k, w2v, precise):
    Bt, n, _ = parts[0].shape
    spec = pl.BlockSpec((1, n, KVH_B * DH), lambda b: (b, 0, 0))
    wspec = pl.BlockSpec((DH, DH), lambda b: (0, 0))
    return pl.pallas_call(
        functools.partial(_cmp_finish_kernel, precise=precise), grid=(Bt,), name='cmp_finish',
        in_specs=[spec] * 4 + [wspec] * 2, out_specs=[spec] * 2,
        out_shape=[jax.ShapeDtypeStruct((Bt, n, KVH_B * DH), F32)] * 2,
        compiler_params=_params('parallel'),
    )(*parts, w2k, w2v)


def _cmp_weight_specs(imap):
    return [pl.BlockSpec((L_CMP, DH), imap), pl.BlockSpec((3 * L_CMP * DH, DH), imap)] * 2


def _cmp_w1_slabs(w1):
    hi = w1.astype(BF16)
    lo = (w1 - hi.astype(F32)).astype(BF16)
    r = lambda t: t.reshape(L_CMP, DH, DH)
    return jnp.concatenate([r(hi), r(hi), r(lo)], axis=1).reshape(3 * L_CMP * DH, DH)


def _compress_prompt(z128, zpl, cw, B, S):
    pek, w1k, w2k, pev, w1v, w2v = cw
    nch = S // S_CMP
    const = lambda b: (0, 0)
    out_spec = pl.BlockSpec((1, nch, KVH_B * DH), lambda b: (b, 0, 0))
    src_spec = pl.BlockSpec((1, nch, S_CMP * 256), lambda b: (b, 0, 0))
    chunked = lambda z, c: z[:, :, c:c + 256].reshape(B, nch, S_CMP * 256)
    parts = pl.pallas_call(
        functools.partial(_cmp_partial_kernel, n_src=1, paged=False, precise=False), grid=(B,), name='cmp_partial_prompt',
        in_specs=[src_spec, src_spec] + _cmp_weight_specs(const),
        out_specs=[out_spec] * 4,
        out_shape=[jax.ShapeDtypeStruct((B, nch, KVH_B * DH), F32)] * 4,
        compiler_params=_params('parallel'),
    )(chunked(z128, 2304), chunked(zpl, 256), pek, _cmp_w1_slabs(w1k), pev, _cmp_w1_slabs(w1v))
    return _cmp_finish(parts, w2k, w2v, precise=False)


def _nsa_prompt_body(qb_ref, kc_ref, vc_ref, ks_ref, vs_ref, kw_ref, vw_ref, gt_ref, ov_ref, ex_ref, o_ref,
                     *, n_cmp, n_slc, n_sel, ext):
    i = pl.program_id(1)
    S = ks_ref.shape[1]
    wspan = WINDOW + Q_BLK
    qpos = i * Q_BLK + lax.broadcasted_iota(I32, (Q_BLK, 1), 0)
    lane = lax.broadcasted_iota(I32, (1, 128), 1)
    kpos = lax.broadcasted_iota(I32, (1, ext), 1)
    causal = kpos <= qpos
    vis = (lane * S_CMP + (L_CMP - 1) <= qpos) & (lane < n_cmp)
    jt = qpos // L_SLC
    forced = (lane == 0) | (lane == jt) | (lane == jt - 1)
    wstart = pl.multiple_of(jnp.clip(i * Q_BLK - WINDOW, 0, S - wspan), Q_BLK)
    wpos = wstart + lax.broadcasted_iota(I32, (1, wspan), 1)
    wmask = (wpos <= qpos) & (wpos > qpos - WINDOW)
    gates = jax.nn.sigmoid(gt_ref[0])
    for kv in range(KVH_B):
        sl = slice(kv * DH, (kv + 1) * DH)
        q = _stack_heads(qb_ref, kv)
        s_cmp = _dg_nt(q, kc_ref[0, :, sl].astype(BF16)) * SCALE
        prs = [_mprobs(s_cmp[g * Q_BLK:(g + 1) * Q_BLK], vis) for g in range(G)]
        o_cmp = _dot(jnp.concatenate(prs, axis=0), vc_ref[0, :, sl])
        imp = jnp.dot(jnp.concatenate(_split(prs[0] + prs[1] + prs[2] + prs[3]), axis=1), ov_ref[...],
                      preferred_element_type=F32)
        sc = jnp.where(lane > jt, NEG, jnp.where(forced, FORCED, imp))
        rank = jnp.zeros((Q_BLK, 128), F32)
        for j2 in range(n_slc):
            col = sc[:, j2:j2 + 1]
            rank = rank + jnp.where(col > sc, 1.0, jnp.where((col == sc) & (lane > j2), 1.0, 0.0))
        sel = jnp.where((rank < n_sel) & (sc > NEG * 0.5), 1.0, 0.0).astype(BF16)
        tokmask = jnp.where(causal, jnp.dot(sel, ex_ref[:, :ext], preferred_element_type=F32), 0.0) > 0.5
        o_slc = _masked_attend(q, ks_ref[0, :ext, sl], vs_ref[0, :ext, sl], tokmask)
        o_win = _masked_attend(q, kw_ref[0, pl.ds(wstart, wspan), sl], vw_ref[0, pl.ds(wstart, wspan), sl], wmask)
        for g in range(G):
            hd = kv * G + g
            rows = slice(g * Q_BLK, (g + 1) * Q_BLK)
            c0 = GATE_OFF + hd * 3
            o_ref[0, :, hd * DH:(hd + 1) * DH] = (gates[:, c0:c0 + 1] * o_cmp[rows] + gates[:, c0 + 1:c0 + 2] * o_slc[rows]
                                                  + gates[:, c0 + 2:c0 + 3] * o_win[rows])


def _overlap_np(nc, nslc, rows, cols):
    cs = np.arange(rows)[:, None] * S_CMP
    js = np.arange(cols)[None, :] * L_SLC
    ov = (cs < js + L_SLC) & (cs + L_CMP > js) & (np.arange(rows)[:, None] < nc) & (np.arange(cols)[None, :] < nslc)
    return jnp.asarray(ov.astype(np.float32), dtype=BF16)


def _nsa_prompt(z128h, zplh, zpl, kc, vc, B, S):
    n_cmp = S // S_CMP - L_CMP // S_CMP + 1
    n_slc = -(-S // L_SLC)
    assert n_cmp <= 128 and n_slc <= 128 and kc.shape[1] == 128
    ov = _overlap_np(n_cmp, n_slc, 128, 128)
    ov = jnp.concatenate([ov, ov], axis=0)
    ex = jnp.asarray((np.arange(128)[:, None] == np.arange(S)[None, :] // L_SLC).astype(np.float32), dtype=BF16)
    blk = lambda w, f: pl.BlockSpec((1, Q_BLK, w), f)
    full = lambda w, c: pl.BlockSpec((1, S, w), lambda b, i: (b, 0, c))
    const = lambda shape: pl.BlockSpec(shape, lambda b, i: (0, 0))
    cspec = pl.BlockSpec((1, 128, 256), lambda b, i: (b, 0, 0))
    return pl.pallas_call(
        _by_key_class(_nsa_prompt_body, S, n_cmp=n_cmp, n_slc=n_slc, n_sel=min(N_SEL_MAX, n_slc)),
        grid=(B, S // Q_BLK), name='nsa_prompt',
        in_specs=[blk(1024, lambda b, i: (b, i, 1)), cspec, cspec, full(256, 10), full(256, 2), full(256, 11), full(256, 3),
                  blk(128, lambda b, i: (b, i, 8)), const((256, 128)), const((128, S))],
        out_specs=blk(1024, lambda b, i: (b, i, 0)),
        out_shape=jax.ShapeDtypeStruct((B, S, H_B * DH), F32),
        compiler_params=_params('parallel', 'parallel'),
    )(z128h, kc, vc, z128h, zplh, z128h, zplh, zpl, ov, ex)


def _merge_u_kernel(*refs, precise):
    oa_ref, ob_ref, wa_ref, wb_ref, ga_ref, gb_ref, o_ref = refs
    mm = _dot3 if precise else _dot
    a, b = mm(oa_ref[...], wa_ref[...]), mm(ob_ref[...], wb_ref[...])
    o_ref[...] = (ga_ref[...] * a + gb_ref[...] * b).astype(o_ref.dtype)


def _merge(x, o_a, o_b, zg, wa, wb, wo, precise=False):
    M = x.shape[0]
    tm, tn = min(M, 512), 512
    nj = D_MODEL // tn
    wspec = lambda k: pl.BlockSpec((k, tn), lambda i, j: (0, j))
    ws = (wa[0], wb[0]) if precise else (wa[1], wb[1])
    wspecs = [wspec(H_A * DH), wspec(H_B * DH)]
    u = pl.pallas_call(
        functools.partial(_merge_u_kernel, precise=precise), grid=(M // tm, nj), name='merge_u',
        in_specs=[pl.BlockSpec((tm, H_A * DH), lambda i, j: (i, 0)), pl.BlockSpec((tm, H_B * DH), lambda i, j: (i, 0))] + wspecs
        + [pl.BlockSpec((tm, tn), lambda i, j: (i, j)), pl.BlockSpec((tm, tn), lambda i, j: (i, j + nj))],
        out_specs=pl.BlockSpec((tm, tn), lambda i, j: (i, j)),
        out_shape=jax.ShapeDtypeStruct((M, D_MODEL), F32 if precise else BF16),
        compiler_params=_params('parallel', 'parallel'),
    )(o_a, o_b, *ws, zg, zg)
    return _mm(u, wo[0] if precise else wo[1], 512, 'res', extra=(x,), precise=precise, name='merge_out')


def _ffn_kernel(te_ref, nu_ref, tr_ref, h_ref, *rest, has_res, n_var, precise):
    nw = 3
    w = rest[:nw]
    o_ref, acc_ref = rest[-2], rest[-1]
    m, f = pl.program_id(0), pl.program_id(1)
    tm = acc_ref.shape[0]
    valid = tr_ref[m]

    @pl.when(f == 0)
    def _():
        acc_ref[...] = jnp.zeros_like(acc_ref)

    def part(rows):
        def run():
            h = h_ref[:rows, :]
            if precise:
                a = _dot3(h, w[0][0])
                b = _dot3(h, w[1][0])
                acc_ref[:rows, :] += _dot3(a * jax.nn.sigmoid(a) * b, w[2][0])
            else:
                a = jnp.dot(h, w[0][0], preferred_element_type=F32)
                b = jnp.dot(h, w[1][0], preferred_element_type=F32)
                acc_ref[:rows, :] += _dot(a * jax.nn.sigmoid(a) * b, w[2][0])
        return run

    step = tm // n_var
    for c in range(n_var):
        pl.when((valid > c * step) & (valid <= (c + 1) * step))(part((c + 1) * step))

    @pl.when(f == pl.num_programs(1) - 1)
    def _():
        o_ref[...] = acc_ref[...] + rest[nw][...] if has_res else acc_ref[...]


def _ffn(h, wg, wu, wd, tile_expert, n_used, tile_rows, res=None, precise=False):
    R, D = h.shape
    F = wg.shape[2]
    tm, tf = min(R, FFN_TM), 512
    nf = F // tf
    fe = lambda m, f, nu: jnp.where(m < nu[0], f, nf - 1)
    up_spec = pl.BlockSpec((1, D, tf), lambda m, f, te, nu, tr: (te[m], 0, fe(m, f, nu)))
    down_spec = pl.BlockSpec((1, tf, D), lambda m, f, te, nu, tr: (te[m], fe(m, f, nu), 0))
    in_specs = [pl.BlockSpec((tm, D), lambda m, f, te, nu, tr: (m, 0)), up_spec, up_spec, down_spec]
    args = [h, wg, wu, wd]
    if res is not None:
        in_specs.append(pl.BlockSpec((tm, D), lambda m, f, te, nu, tr: (m, 0)))
        args.append(res)
    return pl.pallas_call(
        functools.partial(_ffn_kernel, has_res=res is not None, n_var=4 if tm == FFN_TM else 1, precise=precise), name='ffn',
        grid_spec=pltpu.PrefetchScalarGridSpec(
            num_scalar_prefetch=3, grid=(R // tm, nf), in_specs=in_specs,
            out_specs=pl.BlockSpec((tm, D), lambda m, f, te, nu, tr: (m, 0)),
            scratch_shapes=[pltpu.VMEM((tm, D), F32)]),
        out_shape=jax.ShapeDtypeStruct((R, D), F32),
        compiler_params=_params('parallel', 'arbitrary'),
    )(tile_expert, n_used, tile_rows, *args)


def _dense_ffn(x, g, wg, wu, wd, precise=False):
    h = _rms(x, g, F32 if precise else BF16)
    tm = min(x.shape[0], FFN_TM)
    n_tiles = x.shape[0] // tm
    pick = lambda w: w[0 if precise else 1][None]
    return _ffn(h, pick(wg), pick(wu), pick(wd), jnp.zeros((n_tiles,), I32), jnp.full((1,), n_tiles, I32),
                jnp.full((n_tiles,), tm, I32), res=x, precise=precise)


def _router_kernel(x_ref, g_ref, w_ref, h_ref, idx_ref, gate_ref):
    x = x_ref[...]
    h = x * lax.rsqrt(jnp.mean(x * x, axis=-1, keepdims=True) + EPS) * g_ref[...]
    h_ref[...] = h.astype(BF16)
    logits = _dot3(h, w_ref[...])
    lane = lax.broadcasted_iota(I32, logits.shape, 1)
    lanef = lane.astype(F32)
    l1 = jnp.where(lane < N_EXP, logits, -jnp.inf)
    m1 = jnp.max(l1, axis=1, keepdims=True)
    i1 = jnp.min(jnp.where(l1 == m1, lanef, 128.0), axis=1, keepdims=True)
    l2 = jnp.where(lanef == i1, -jnp.inf, l1)
    m2 = jnp.max(l2, axis=1, keepdims=True)
    i2 = jnp.min(jnp.where(l2 == m2, lanef, 128.0), axis=1, keepdims=True)
    e = jnp.exp(m2 - m1)
    inv = 1.0 / (1.0 + e)
    idx_ref[...] = jnp.where(lane == 0, i1, i2).astype(I32)
    gate_ref[...] = jnp.where(lane == 0, inv, e * inv)


def _router(x, g, w_r):
    M, D = x.shape
    tm = min(M, 512)
    wpad = jnp.pad(w_r, ((0, 0), (0, 128 - N_EXP)))
    row = lambda w, dt: (pl.BlockSpec((tm, w), lambda i: (i, 0)), jax.ShapeDtypeStruct((M, w), dt))
    outs = [row(D, BF16), row(128, I32), row(128, F32)]
    h, idx, gate = pl.pallas_call(
        _router_kernel, grid=(M // tm,), name='router',
        in_specs=[pl.BlockSpec((tm, D), lambda i: (i, 0)), pl.BlockSpec((1, D), lambda i: (0, 0)),
                  pl.BlockSpec((D, 128), lambda i: (0, 0))],
        out_specs=[o[0] for o in outs], out_shape=[o[1] for o in outs],
        compiler_params=_params('parallel'),
    )(x, g.reshape(1, D), wpad)
    return h, idx[:, :2], gate[:, :2]


def _moe(xp, xs, g, w_r, wg, wu, wd):
    hp, ip, gp = _router(xp, g, w_r)
    hs, is_, gs = _router(xs, g, w_r)
    h = jnp.concatenate([hp, hs], axis=0)
    idx = jnp.concatenate([ip, is_], axis=0)
    gate = jnp.concatenate([gp, gs], axis=0)
    M = h.shape[0]
    tm = FFN_TM
    n_tiles = (2 * M + N_EXP * (tm - 1) + tm - 1) // tm
    e = jnp.concatenate([idx[:, 0], idx[:, 1]])
    onehot = (e[:, None] == jnp.arange(N_EXP, dtype=I32)[None, :]).astype(I32)
    rank = jnp.sum((jnp.cumsum(onehot, axis=0) - onehot) * onehot, axis=1)
    counts = jnp.sum(onehot, axis=0)
    tiles_per = (counts + tm - 1) // tm
    tile_end = jnp.cumsum(tiles_per)
    row_start = (tile_end - tiles_per) * tm
    pos = jnp.sum(onehot * row_start[None, :], axis=1) + rank
    src_tok = jnp.zeros((n_tiles * tm,), I32).at[pos].set(jnp.arange(2 * M, dtype=I32) % M)
    n_used = tile_end[-1].astype(I32)
    tid = jnp.arange(n_tiles, dtype=I32)
    t = jnp.minimum(tid, n_used - 1)
    tile_expert = jnp.sum((t[:, None] >= tile_end[None, :]).astype(I32), axis=1)
    rows_left = counts[tile_expert] - (t - (tile_end - tiles_per)[tile_expert]) * tm
    tile_rows = jnp.where(tid < n_used, jnp.clip(rows_left, 0, tm), 0).astype(I32)
    y = _ffn(jnp.take(h, src_tok, axis=0), wg, wu, wd, tile_expert, n_used.reshape(1), tile_rows)
    f = gate[:, 0:1] * jnp.take(y, pos[:M], axis=0) + gate[:, 1:2] * jnp.take(y, pos[M:], axis=0)
    Mp = xp.shape[0]
    return xp + f[:Mp], xs + f[Mp:]


def _page_specs(block, n_lead, layer, col=None):
    def spec(j):
        def imap(b, s, pt):
            page = pt[b * (PAST_LEN // PAGE_SIZE) + s * PAGES_PER_STEP + j]
            return (layer, page) + (0,) * n_lead
        return pl.BlockSpec(block, imap)
    return [spec(j) for j in range(PAGES_PER_STEP)]


def _idx_walk_kernel(pt_ref, q_ref, w_ref, *refs):
    pages, o_ref = refs[:-1], refs[-1]
    keys = jnp.concatenate([r[0, 0] for r in pages], axis=0)
    d = _dot3_nt(q_ref[0], keys) * (D_IDX ** -0.5)
    o_ref[0, 0] = jnp.sum(jnp.maximum(d, 0.0) * (w_ref[0] * (H_I ** -0.5)), axis=0, keepdims=True)


def _dsa_thresh_kernel(sc_ref, qi_ref, wi_ref, kin_ref, sel_ref, selnew_ref, *, k_sel):
    nb = sc_ref.shape[0]
    snew = []
    for b in range(nb):
        d = jnp.sum(qi_ref[b] * kin_ref[b:b + 1, :], axis=1, keepdims=True) * (D_IDX ** -0.5)
        snew.append(jnp.sum(jnp.maximum(d, 0.0) * (wi_ref[b] * (H_I ** -0.5)), axis=0, keepdims=True))
    snew = jnp.concatenate(snew, axis=0)
    key, knew = _order_key(sc_ref[...]), _order_key(snew)
    count_ge = lambda c: (jnp.sum(jnp.where(key >= c, 1.0, 0.0), axis=1, keepdims=True) + jnp.where(knew >= c, 1.0, 0.0))
    thr = _kth_largest_key(count_ge, nb, k_sel)
    sel_ref[...] = jnp.where((key >= thr) & (sc_ref[...] > NEG * 0.5), 1.0, 0.0)
    selnew_ref[...] = jnp.broadcast_to(jnp.where((knew >= thr) & (snew > NEG * 0.5), 1.0, 0.0), selnew_ref.shape)


def _dsa_walk_kernel(pt_ref, q_ref, sel_ref, selnew_ref, knew_ref, vnew_ref, *refs):
    n = PAGES_PER_STEP
    kpages, vpages = refs[:n], refs[n:2 * n]
    o_ref, m_ref, l_ref, acc_ref = refs[2 * n:]
    s_id = pl.program_id(1)
    row = lax.broadcasted_iota(I32, (H_A, 1), 0)
    first = row < G

    @pl.when(s_id == 0)
    def _():
        m_ref[...] = jnp.full_like(m_ref, NEG)
        l_ref[...] = jnp.zeros_like(l_ref)
        acc_ref[...] = jnp.zeros_like(acc_ref)

    q = q_ref[0]
    k = jnp.concatenate([r[0, 0] for r in kpages], axis=0)
    v = jnp.concatenate([r[0, 0] for r in vpages], axis=0)
    col = lax.broadcasted_iota(I32, (1, k.shape[0]), 1)
    mask = jnp.where((col & 1) == jnp.where(first, 0, 1), sel_ref[0, 0], 0.0) > 0.5
    s = jnp.where(mask, _dot3_nt(q, k) * SCALE, NEG)
    m_new = jnp.maximum(m_ref[...], jnp.max(s, axis=1, keepdims=True))
    p = jnp.where(mask, jnp.exp(s - m_new), 0.0)
    alpha = jnp.exp(m_ref[...] - m_new)
    l_ref[...] = alpha * l_ref[...] + jnp.sum(p, axis=1, keepdims=True)
    acc_ref[...] = alpha * acc_ref[...] + _dot3(p, v)
    m_ref[...] = m_new

    @pl.when(s_id == pl.num_programs(1) - 1)
    def _():
        kn = jnp.where(first, knew_ref[0][:, :DH], knew_ref[0][:, DH:])
        vn = jnp.where(first, vnew_ref[0][:, :DH], vnew_ref[0][:, DH:])
        valid = selnew_ref[0][:, :1] > 0.5
        sn = jnp.where(valid, jnp.sum(q * kn, axis=1, keepdims=True) * SCALE, NEG)
        m_fin = jnp.maximum(m_ref[...], sn)
        pn = jnp.where(valid, jnp.exp(sn - m_fin), 0.0)
        a2 = jnp.exp(m_ref[...] - m_fin)
        l_fin = a2 * l_ref[...] + pn
        o_ref[0] = (a2 * acc_ref[...] + pn * vn) * (1.0 / jnp.maximum(l_fin, 1e-30))


def _dsa_sample(l, z128, z64, zpl, cache_k, cache_v, cache_ki, pt_flat, DB):
    n_pages = PAST_LEN // PAGE_SIZE
    n_steps = n_pages // PAGES_PER_STEP
    span = PAGES_PER_STEP * PAGE_SIZE
    k_sel = min(TOPK_MAX, (PAST_LEN + 1) // 4)
    qi = z64[:DB, :H_I * D_IDX].reshape(DB, H_I, D_IDX)
    wi = zpl[:DB, 1024:1024 + H_I].reshape(DB, H_I, 1)
    kin = z64[:DB, 1024:1024 + D_IDX]
    scores = pl.pallas_call(
        _idx_walk_kernel, name='idx_walk',
        grid_spec=pltpu.PrefetchScalarGridSpec(
            num_scalar_prefetch=1, grid=(DB, n_steps),
            in_specs=[pl.BlockSpec((1, H_I, D_IDX), lambda b, s, pt: (b, 0, 0)), pl.BlockSpec((1, H_I, 1), lambda b, s, pt: (b, 0, 0))]
            + _page_specs((1, 1, PAGE_SIZE, D_IDX), 2, l),
            out_specs=pl.BlockSpec((1, 1, 1, span), lambda b, s, pt: (b, s, 0, 0))),
        out_shape=jax.ShapeDtypeStruct((DB, n_steps, 1, span), F32),
        compiler_params=_params('parallel', 'parallel'),
    )(pt_flat, qi, wi, *([cache_ki] * PAGES_PER_STEP))
    sel, selnew = pl.pallas_call(
        functools.partial(_dsa_thresh_kernel, k_sel=k_sel), name='dsa_thresh',
        out_shape=[jax.ShapeDtypeStruct((DB, PAST_LEN), F32), jax.ShapeDtypeStruct((DB, 128), F32)],
        compiler_params=pltpu.CompilerParams(vmem_limit_bytes=VMEM_LIMIT),
    )(scores.reshape(DB, PAST_LEN), qi, wi, kin)
    qa = z128[:DB, :H_A * DH].reshape(DB, H_A, DH)
    knew = jnp.broadcast_to(z128[:DB, None, 2048:2304], (DB, H_A, KVH_A * DH))
    vnew = jnp.broadcast_to(zpl[:DB, None, 0:256], (DB, H_A, KVH_A * DH))
    selnew = jnp.broadcast_to(selnew[:, None, :], (DB, H_A, 128))
    per_b = lambda shape: pl.BlockSpec(shape, lambda b, s, pt: (b, 0, 0))
    o = pl.pallas_call(
        _dsa_walk_kernel, name='dsa_walk',
        grid_spec=pltpu.PrefetchScalarGridSpec(
            num_scalar_prefetch=1, grid=(DB, n_steps),
            in_specs=[per_b((1, H_A, DH)), pl.BlockSpec((1, 1, 1, KVH_A * span), lambda b, s, pt: (b, s, 0, 0)),
                      per_b((1, H_A, 128)), per_b((1, H_A, KVH_A * DH)), per_b((1, H_A, KVH_A * DH))]
            + _page_specs((1, 1, KVH_A * PAGE_SIZE, DH), 2, l) * 2,
            out_specs=per_b((1, H_A, DH)),
            scratch_shapes=[pltpu.VMEM((H_A, 1), F32), pltpu.VMEM((H_A, 1), F32), pltpu.VMEM((H_A, DH), F32)]),
        out_shape=jax.ShapeDtypeStruct((DB, H_A, DH), F32),
        compiler_params=_params('parallel', 'arbitrary'),
    )(pt_flat, qa, jnp.repeat(sel, KVH_A, axis=1).reshape(DB, n_steps, 1, KVH_A * span), selnew, knew, vnew,
      *([cache_k] * PAGES_PER_STEP), *([cache_v] * PAGES_PER_STEP))
    return o.reshape(DB, H_A * DH)


def _compress_sample(l, cache_k, cache_v, cw, pt_flat, DB):
    pek, w1k, w2k, pev, w1v, w2v = cw
    n_pages = PAST_LEN // PAGE_SIZE
    n_steps = n_pages // PAGES_PER_STEP
    cps = PAGE_SIZE // S_CMP
    nch_step = PAGES_PER_STEP * cps
    const = lambda b, s, pt: (0, 0)
    out_spec = pl.BlockSpec((1, nch_step, KVH_B * DH), lambda b, s, pt: (b, s, 0))
    page_specs = _page_specs((1, 1, KVH_B * PAGE_SIZE, DH), 2, l)
    parts = pl.pallas_call(
        functools.partial(_cmp_partial_kernel, n_src=PAGES_PER_STEP, paged=True, precise=True), name='cmp_partial_sample',
        grid_spec=pltpu.PrefetchScalarGridSpec(
            num_scalar_prefetch=1, grid=(DB, n_steps),
            in_specs=page_specs + page_specs + _cmp_weight_specs(const),
            out_specs=[out_spec] * 4),
        out_shape=[jax.ShapeDtypeStruct((DB, PAST_LEN // S_CMP, KVH_B * DH), F32)] * 4,
        compiler_params=_params('parallel', 'parallel'),
    )(pt_flat, *([cache_k] * PAGES_PER_STEP), *([cache_v] * PAGES_PER_STEP), pek, _cmp_w1_slabs(w1k), pev, _cmp_w1_slabs(w1v))
    return _cmp_finish(parts, w2k, w2v, precise=True)


def _cmp_select_sample_kernel(q_ref, kc_ref, vc_ref, ov_ref, o_ref, ids_ref, *, n_cmp, n_slc, n_sel):
    n = kc_ref.shape[1]
    q = q_ref[0]
    row = lax.broadcasted_iota(I32, (H_B, 1), 0)
    first = row < G
    kc = kc_ref[0]
    vc = vc_ref[0]
    s = jnp.where(first, _dot3_nt(q, kc[:, :DH]), _dot3_nt(q, kc[:, DH:])) * SCALE
    col = lax.broadcasted_iota(I32, (1, n), 1)
    qpos = PAST_LEN
    pr = _mprobs(s, (col * S_CMP + (L_CMP - 1) <= qpos) & (col < n_cmp))
    o_ref[0] = jnp.where(first, _dot3(pr, vc[:, :DH]), _dot3(pr, vc[:, DH:]))
    psum = jnp.concatenate([jnp.sum(jnp.where((row >= kv * G) & (row < (kv + 1) * G), pr, 0.0), axis=0, keepdims=True)
                            for kv in range(KVH_B)] + [jnp.zeros((8 - KVH_B, n), F32)], axis=0)
    ph, plo = _split(psum)
    imp = jnp.dot(ph, ov_ref[...], preferred_element_type=F32) + jnp.dot(plo, ov_ref[...], preferred_element_type=F32)
    j = lax.broadcasted_iota(I32, imp.shape, 1)
    jf = j.astype(F32)
    jt = qpos // L_SLC
    forced = (j == 0) | (j == jt) | (j == jt - 1)
    sc = jnp.where((j > jt) | (j >= n_slc), NEG, jnp.where(forced, FORCED, imp))
    lane = lax.broadcasted_iota(I32, (8, 128), 1)
    ids = jnp.zeros((8, 128), F32)
    for r in range(n_sel):
        m = jnp.max(sc, axis=1, keepdims=True)
        pick = jnp.min(jnp.where(sc == m, jf, 1e9), axis=1, keepdims=True)
        ids = jnp.where(lane == r, pick, ids)
        sc = jnp.where(jf == pick, -jnp.inf, sc)
    ids_ref[0] = ids.astype(I32)


def _slc_sample_kernel(ids_ref, pt_ref, q_ref, knew_ref, vnew_ref, *refs, n_sel, n_past):
    kblks, vblks, o_ref = refs[:n_sel], refs[n_sel:2 * n_sel], refs[2 * n_sel]
    b, kv = pl.program_id(0), pl.program_id(1)
    base = (b * KVH_B + kv) * n_sel
    nrow = KVH_B * L_SLC
    row = lax.broadcasted_iota(I32, (nrow, 1), 0)
    lane = lax.broadcasted_iota(I32, (1, n_sel * nrow), 1)
    tpos = (lane % nrow) // KVH_B
    ks, vs = [], []
    for r in range(n_sel):
        bid = ids_ref[base + r]
        ks.append(jnp.where(bid < n_past, kblks[r][0, 0, 0], jnp.where(row == kv, knew_ref[0, 0], 0.0)))
        vs.append(jnp.where(bid < n_past, vblks[r][0, 0, 0], jnp.where(row == kv, vnew_ref[0, 0], 0.0)))
        tpos = tpos + jnp.where(lane // nrow == r, bid * L_SLC, 0)
    mask = jnp.where(lane % KVH_B == kv, tpos, PAST_LEN + 1) <= PAST_LEN
    s = _dot3_nt(q_ref[0, 0], jnp.concatenate(ks, axis=0)) * SCALE
    o_ref[0, 0] = _dot3(_mprobs(s, mask), jnp.concatenate(vs, axis=0))


def _win_sample_kernel(q_ref, kb_ref, vb_ref, kn_ref, vn_ref, o_ref, ko_ref, vo_ref):
    W = kb_ref.shape[2]
    q = q_ref[0]
    rowq = lax.broadcasted_iota(I32, (H_B, 1), 0)
    first = rowq < G
    kb, vb = kb_ref[0, 0], vb_ref[0, 0]
    kn, vn = kn_ref[0], vn_ref[0]
    s = jnp.where(first, _dot3_nt(q, kb[:, :DH]), _dot3_nt(q, kb[:, DH:])) * SCALE
    col = lax.broadcasted_iota(I32, (1, W), 1)
    mask = (PAST_LEN - W + col) > (PAST_LEN - WINDOW)
    knh = jnp.where(first, kn[:, :DH], kn[:, DH:])
    vnh = jnp.where(first, vn[:, :DH], vn[:, DH:])
    sn = jnp.sum(q * knh, axis=1, keepdims=True) * SCALE
    s = jnp.where(mask, s, NEG)
    m = jnp.maximum(jnp.max(s, axis=1, keepdims=True), sn)
    p = jnp.where(mask, jnp.exp(s - m), 0.0)
    pn = jnp.exp(sn - m)
    inv = 1.0 / jnp.maximum(jnp.sum(p, axis=1, keepdims=True) + pn, 1e-30)
    pn_ = p * inv
    pv = jnp.where(first, _dot3(pn_, vb[:, :DH]), _dot3(pn_, vb[:, DH:]))
    o_ref[0] = pv + (pn * inv) * vnh
    roww = lax.broadcasted_iota(I32, (W, 1), 0)
    ko_ref[0] = jnp.where(roww == W - 1, kn[:1, :], pltpu.roll(kb, W - 1, 0))
    vo_ref[0] = jnp.where(roww == W - 1, vn[:1, :], pltpu.roll(vb, W - 1, 0))


def _nsa_sample(l, z128, zpl, cache_ck, cache_cv, cache_sk, cache_sv, buf_k, buf_v, cw, pt_flat, DB):
    kc, vc = _compress_sample(l, cache_ck, cache_cv, cw, pt_flat, DB)
    n = kc.shape[1]
    L = PAST_LEN + 1
    n_cmp = L // S_CMP - L_CMP // S_CMP + 1
    n_slc = -(-L // L_SLC)
    n_sel = min(N_SEL_MAX, n_slc)
    n_past = PAST_LEN // L_SLC
    cols = -(-n_slc // 128) * 128
    qb = z128[:DB, 1024:2048].reshape(DB, H_B, DH)
    per_b = lambda shape: pl.BlockSpec(shape, lambda b: (b,) + (0,) * (len(shape) - 1))
    o_cmp, ids = pl.pallas_call(
        functools.partial(_cmp_select_sample_kernel, n_cmp=n_cmp, n_slc=n_slc, n_sel=n_sel), grid=(DB,), name='cmp_select_sample',
        in_specs=[per_b((1, H_B, DH)), per_b((1, n, 256)), per_b((1, n, 256)), pl.BlockSpec((n, cols), lambda b: (0, 0))],
        out_specs=[per_b((1, H_B, DH)), per_b((1, 8, 128))],
        out_shape=[jax.ShapeDtypeStruct((DB, H_B, DH), F32), jax.ShapeDtypeStruct((DB, 8, 128), I32)],
        compiler_params=_params('parallel'),
    )(qb, kc, vc, _overlap_np(n_cmp, n_slc, n, cols))
    ids_flat = ids[:, :KVH_B, :n_sel].reshape(-1)
    q4 = jnp.pad(qb.reshape(DB, KVH_B, G, DH), ((0, 0), (0, 0), (0, 8 - G), (0, 0)))
    knew = z128[:DB, 2560:2816].reshape(DB, KVH_B, 1, DH)
    vnew = zpl[:DB, 512:768].reshape(DB, KVH_B, 1, DH)
    bpp = PAGE_SIZE // L_SLC
    n_pages = PAST_LEN // PAGE_SIZE

    nrow = KVH_B * L_SLC

    def blk_spec(r):
        def imap(b, kv, ids_s, pt):
            bid = jnp.minimum(ids_s[(b * KVH_B + kv) * n_sel + r], n_past - 1)
            return (l, pt[b * n_pages + bid // bpp], bid % bpp, 0, 0)
        return pl.BlockSpec((1, 1, 1, nrow, DH), imap)

    per_bk = lambda shape: pl.BlockSpec(shape, lambda b, kv, ids_s, pt: (b, kv, 0, 0))
    view = lambda c: c.reshape(c.shape[0], c.shape[1], bpp, nrow, DH)
    blk_specs = [blk_spec(r) for r in range(n_sel)]
    o_slc = pl.pallas_call(
        functools.partial(_slc_sample_kernel, n_sel=n_sel, n_past=n_past), name='slc_sample',
        grid_spec=pltpu.PrefetchScalarGridSpec(
            num_scalar_prefetch=2, grid=(DB, KVH_B),
            in_specs=[per_bk((1, 1, 8, DH)), per_bk((1, 1, 1, DH)), per_bk((1, 1, 1, DH))] + blk_specs + blk_specs,
            out_specs=per_bk((1, 1, 8, DH))),
        out_shape=jax.ShapeDtypeStruct((DB, KVH_B, 8, DH), F32),
        compiler_params=_params('parallel', 'parallel'),
    )(ids_flat, pt_flat, q4, knew, vnew, *([view(cache_sk)] * n_sel), *([view(cache_sv)] * n_sel))
    W = buf_k.shape[2]
    kn8 = jnp.broadcast_to(z128[:DB, None, 2816:3072], (DB, 8, 256))
    vn8 = jnp.broadcast_to(zpl[:DB, None, 768:1024], (DB, 8, 256))
    bufspec = pl.BlockSpec((1, 1, W, 256), lambda b: (l, b, 0, 0))
    o_win, nk, nv = pl.pallas_call(
        _win_sample_kernel, grid=(DB,), name='win_sample',
        in_specs=[per_b((1, H_B, DH)), bufspec, bufspec, per_b((1, 8, 256)), per_b((1, 8, 256))],
        out_specs=[per_b((1, H_B, DH)), per_b((1, W, 256)), per_b((1, W, 256))],
        out_shape=[jax.ShapeDtypeStruct((DB, H_B, DH), F32)] + [jax.ShapeDtypeStruct((DB, W, 256), F32)] * 2,
        compiler_params=_params('parallel'),
    )(qb, buf_k.reshape(buf_k.shape[0], DB, W, 256), buf_v.reshape(buf_v.shape[0], DB, W, 256), kn8, vn8)
    gb = jax.nn.sigmoid(zpl[:DB, 1024 + GATE_OFF:1024 + GATE_OFF + 3 * H_B]).reshape(DB, H_B, 3)
    o_b = gb[..., 0:1] * o_cmp + gb[..., 1:2] * o_slc[:, :, :G].reshape(DB, H_B, DH) + gb[..., 2:3] * o_win
    return o_b.reshape(DB, H_B * DH), nk, nv


def kernel(x_prompt, x_sample, cache_dsa_k, cache_dsa_v, cache_idx_k, cache_cmp_k, cache_cmp_v, cache_slc_k, cache_slc_v, state_win_k, state_win_v, page_table, norm_mix, w_in, cmp_pe_k, cmp_w1_k, cmp_w2_k, cmp_pe_v, cmp_w1_v, cmp_w2_v, w_up_a, w_up_b, w_o, norm_ffn, dense_w_gate, dense_w_up, dense_w_down, moe_w_router, moe_w_gate, moe_w_up, moe_w_down, norm_final):
    B, S, D = x_prompt.shape
    DB, T, _ = x_sample.shape
    depth = w_in.shape[0]
    assert T == 1 and D == D_MODEL and page_table.shape == (DB, PAST_LEN // PAGE_SIZE)
    wp = min(WINDOW, S)
    xp = x_prompt.reshape(B * S, D)
    xs = jnp.pad(x_sample.reshape(DB, D), ((0, SAMPLE_ROWS - DB), (0, 0)))
    pt_flat = page_table.reshape(-1).astype(I32)
    pos_p = jnp.tile(jnp.arange(S), B)
    pos_s = jnp.full((SAMPLE_ROWS,), PAST_LEN)
    tabs_p = (_rope_tables(pos_p, ROT_DIM, DH), _rope_tables(pos_p, ROT_IDX, D_IDX))
    tabs_s = (_rope_tables(pos_s, ROT_DIM, DH), _rope_tables(pos_s, ROT_IDX, D_IDX))
    rows_view = lambda c: c.reshape(c.shape[0], c.shape[1], -1, c.shape[-1])
    c_dsa_k, c_dsa_v, c_idx, c_cmp_k, c_cmp_v, c_slc_k, c_slc_v = map(
        rows_view, (cache_dsa_k, cache_dsa_v, cache_idx_k, cache_cmp_k, cache_cmp_v, cache_slc_k, cache_slc_v))
    newp, news = [], []
    for l in range(depth):
        cw = (cmp_pe_k[l], cmp_w1_k[l], cmp_w2_k[l], cmp_pe_v[l], cmp_w1_v[l], cmp_w2_v[l])
        wsplit = _split_w_in(w_in[l])
        wa, wb, wo = _with_bf16(w_up_a[l]), _with_bf16(w_up_b[l]), _with_bf16(w_o[l])

        z128, z64, zpl, zg = _project(_rms(xp, norm_mix[l], BF16), wsplit, *tabs_p)
        (z128b, z128h), (z64b, z64h, z64l), (zplb, zplh) = (tuple(t.reshape(B, S, -1) for t in z) for z in (z128, z64, zpl))
        o_a = _dsa_prompt(z128h, z64h, z64l, zplh, zplb, B, S)
        kc, vc = _compress_prompt(z128b, zplb, cw, B, S)
        o_b = _nsa_prompt(z128h, zplh, zplb, kc, vc, B, S)
        xp = _merge(xp, o_a.reshape(B * S, -1), o_b.reshape(B * S, -1), zg, wa, wb, wo)
        kv4 = lambda z, c: z[:, :, c:c + 256].reshape(B, S, 2, DH)
        newp.append((kv4(z128b, 2048), kv4(zplb, 0), z64b[:, :, 1024:1024 + D_IDX], kv4(z128b, 2304), kv4(zplb, 256),
                     kv4(z128b, 2560), kv4(zplb, 512), kv4(z128b, 2816)[:, S - wp:], kv4(zplb, 768)[:, S - wp:]))

        y128, y64, ypl, yg = _project(_rms(xs, norm_mix[l], F32), wsplit, *tabs_s, precise=True)
        o_a = _dsa_sample(l, y128, y64, ypl, c_dsa_k, c_dsa_v, c_idx, pt_flat, DB)
        o_b, win_k, win_v = _nsa_sample(l, y128, ypl, c_cmp_k, c_cmp_v, c_slc_k, c_slc_v, state_win_k, state_win_v, cw, pt_flat, DB)
        pad = lambda o: jnp.pad(o, ((0, SAMPLE_ROWS - DB), (0, 0)))
        xs = _merge(xs, pad(o_a), pad(o_b), yg, wa, wb, wo, precise=True)
        s4 = lambda z, c: z[:DB, c:c + 256].reshape(DB, 1, 2, DH)
        news.append((s4(y128, 2048), s4(ypl, 0), y64[:DB, 1024:1024 + D_IDX].reshape(DB, 1, D_IDX), s4(y128, 2304), s4(ypl, 256),
                     s4(y128, 2560), s4(ypl, 512), win_k.reshape(DB, -1, 2, DH), win_v.reshape(DB, -1, 2, DH)))

        i = l // 2
        if l % 2 == 0:
            ws = (_with_bf16(dense_w_gate[i]), _with_bf16(dense_w_up[i]), _with_bf16(dense_w_down[i]))
            xp = _dense_ffn(xp, norm_ffn[l], *ws)
            xs = _dense_ffn(xs, norm_ffn[l], *ws, precise=True)
        else:
            xp, xs = _moe(xp, xs, norm_ffn[l], moe_w_router[i], moe_w_gate[i].astype(BF16), moe_w_up[i].astype(BF16),
                          moe_w_down[i].astype(BF16))

    y_prompt = _rms(xp, norm_final, F32).reshape(B, S, D)
    y_sample = _rms(xs, norm_final, F32)[:DB].reshape(DB, 1, D)
    stack = lambda items: [jnp.stack([it[n] for it in items], axis=0) for n in range(9)]
    return (y_prompt, y_sample, *stack(newp), *stack(news))
```

```python
import functools

import numpy as np
import jax
import jax.numpy as jnp
from jax import lax
from jax.experimental import pallas as pl
from jax.experimental.pallas import tpu as pltpu

F32 = jnp.float32
BF16 = jnp.bfloat16
I32 = jnp.int32

D_MODEL = 2048
PAST_LEN = 16384
PAGE_SIZE = 128
DH = 128
ROT_DIM = DH // 4
ROPE_THETA = 500000.0
H_A = 8
KVH_A = 2
H_I = 16
D_IDX = 64
ROT_IDX = D_IDX // 4
TOPK_MAX = 256
H_B = 8
KVH_B = 2
G = 4
L_CMP = 32
S_CMP = 16
L_SLC = 64
N_SEL_MAX = 16
WINDOW = 512
N_EXP = 8
Q_BLK = 128
FFN_TM = 512
KEY_CLS = 512
EPS = 1e-6
NEG = -1e30
FORCED = 1e6
INT_MIN = -2 ** 31
SCALE = DH ** -0.5
PAGES_PER_STEP = 16
SAMPLE_ROWS = 16
VMEM_LIMIT = 48 * 1024 * 1024

_IN_SPLITS = (
    ('q_a', H_A * DH), ('k_a', KVH_A * DH), ('v_a', KVH_A * DH),
    ('q_i', H_I * D_IDX), ('k_i', D_IDX), ('w_i', H_I),
    ('q_b', H_B * DH),
    ('k_cmp', KVH_B * DH), ('v_cmp', KVH_B * DH),
    ('k_slc', KVH_B * DH), ('v_slc', KVH_B * DH),
    ('k_win', KVH_B * DH), ('v_win', KVH_B * DH),
    ('g_b', 3 * H_B), ('g_mix', 2 * D_MODEL),
)
_GROUP_ROPE128 = ('q_a', 'q_b', 'k_a', 'k_cmp', 'k_slc', 'k_win')
_GROUP_ROPE64 = ('q_i', 'k_i')
_GROUP_PLAIN = ('v_a', 'v_cmp', 'v_slc', 'v_win', 'w_i', 'g_b')
W_R64 = 1280
W_PLAIN = 1280
GATE_OFF = H_I


def _params(*sem):
    return pltpu.CompilerParams(dimension_semantics=sem, vmem_limit_bytes=VMEM_LIMIT)


def _dot(a, b):
    return jnp.dot(a.astype(BF16), b.astype(BF16), preferred_element_type=F32)


def _dg_nt(a, b):
    return lax.dot_general(a, b, (((1,), (1,)), ((), ())), preferred_element_type=F32)


def _dot_nt(a, b):
    return _dg_nt(a.astype(BF16), b.astype(BF16))


def _split(a):
    hi = a.astype(BF16)
    lo = (a - hi.astype(F32)).astype(BF16)
    return hi, lo


def _dot3_nt(a, b):
    ah, al = _split(a)
    bh, bl = _split(b)
    return _dg_nt(jnp.concatenate([ah, ah, al], axis=1), jnp.concatenate([bh, bl, bh], axis=1))


def _dot3(a, b):
    ah, al = _split(a)
    bh, bl = _split(b)
    return jnp.dot(jnp.concatenate([ah, ah, al], axis=1), jnp.concatenate([bh, bl, bh], axis=0), preferred_element_type=F32)


def _with_bf16(w):
    return w, w.astype(BF16)


def _mprobs(s, mask):
    s = jnp.where(mask, s, NEG)
    m = jnp.max(s, axis=-1, keepdims=True)
    p = jnp.where(mask, jnp.exp(s - m), 0.0)
    return p * (1.0 / jnp.maximum(jnp.sum(p, axis=-1, keepdims=True), 1e-30))


def _order_key(x):
    b = lax.bitcast_convert_type(x + 0.0, I32)
    return jnp.where(b < 0, b ^ jnp.int32(0x7FFFFFFF), b)


def _kth_largest_key(count_ge, rows, k, n_bits=32):
    def body(it, lo):
        cand = lo + jnp.left_shift(jnp.int32(1), 31 - it)
        return jnp.where(count_ge(cand) >= k, cand, lo)
    return lax.fori_loop(0, n_bits, body, jnp.full((rows, 1), INT_MIN, I32))


def _rms_kernel(x_ref, g_ref, o_ref):
    x = x_ref[...]
    y = x * lax.rsqrt(jnp.mean(x * x, axis=-1, keepdims=True) + EPS)
    o_ref[...] = (y * g_ref[...]).astype(o_ref.dtype)


def _rms(x, g, out_dtype):
    M, D = x.shape
    tm = min(M, 512)
    return pl.pallas_call(
        _rms_kernel, grid=(M // tm,), name='rms',
        in_specs=[pl.BlockSpec((tm, D), lambda i: (i, 0)), pl.BlockSpec((1, D), lambda i: (0, 0))],
        out_specs=pl.BlockSpec((tm, D), lambda i: (i, 0)),
        out_shape=jax.ShapeDtypeStruct((M, D), out_dtype),
        compiler_params=_params('parallel'),
    )(x, g.reshape(1, D))


def _mm_kernel(*refs, mode, shift, out_kinds, precise):
    if precise:
        acc = _dot3(refs[0][...], refs[1][...])
    else:
        acc = jnp.dot(refs[0][...], refs[1][...], preferred_element_type=F32)
    if mode == 'rope':
        c_ref, s1_ref, s2_ref = refs[2:5]
        tn = acc.shape[1]
        rep = tn // 128
        til = lambda r: jnp.concatenate([r[...]] * rep, axis=1)
        acc = (acc * til(c_ref) + pltpu.roll(acc, tn - shift, 1) * til(s1_ref)
               + pltpu.roll(acc, shift, 1) * til(s2_ref))
    elif mode == 'sigmoid':
        acc = jax.nn.sigmoid(acc)
    elif mode == 'res':
        acc = acc + refs[2][...]
    for kind, o_ref in zip(out_kinds, refs[len(refs) - len(out_kinds):]):
        if kind == 'lo':
            o_ref[...] = (acc - acc.astype(BF16).astype(F32)).astype(BF16)
        else:
            o_ref[...] = acc.astype(o_ref.dtype)


_KIND_DTYPE = {'f32': F32, 'bf16': BF16, 'lo': BF16}


def _mm(a, b, tn, mode='none', shift=0, extra=(), out_kinds=('f32',), precise=False, name='mm'):
    out_dtypes = tuple(_KIND_DTYPE[k] for k in out_kinds)
    M, K = a.shape
    N = b.shape[1]
    tm = min(M, 1024)
    in_specs = [pl.BlockSpec((tm, K), lambda i, j: (i, 0)), pl.BlockSpec((K, tn), lambda i, j: (0, j))]
    if mode == 'rope':
        in_specs += [pl.BlockSpec((tm, 128), lambda i, j: (i, 0))] * 3
    elif mode == 'res':
        in_specs += [pl.BlockSpec((tm, tn), lambda i, j: (i, j))]
    outs = pl.pallas_call(
        functools.partial(_mm_kernel, mode=mode, shift=shift, out_kinds=out_kinds, precise=precise),
        grid=(M // tm, N // tn), name=name,
        in_specs=in_specs, out_specs=[pl.BlockSpec((tm, tn), lambda i, j: (i, j))] * len(out_dtypes),
        out_shape=[jax.ShapeDtypeStruct((M, N), dt) for dt in out_dtypes],
        compiler_params=_params('parallel', 'parallel'),
    )(a, b, *extra)
    return outs[0] if len(out_dtypes) == 1 else tuple(outs)


def _rope_tables(pos, rot, period):
    half = rot // 2
    inv = jnp.power(jnp.float32(ROPE_THETA), -jnp.arange(half, dtype=F32) / half)
    ang = pos.astype(F32)[:, None] * inv[None, :]
    cos, sin = jnp.cos(ang), jnp.sin(ang)
    T = pos.shape[0]
    z = jnp.zeros((T, period - rot), F32)
    zh = jnp.zeros((T, half), F32)
    c = jnp.concatenate([cos, cos, jnp.ones((T, period - rot), F32)], axis=1)
    s1 = jnp.concatenate([-sin, zh, z], axis=1)
    s2 = jnp.concatenate([zh, sin, z], axis=1)
    rep = 128 // period
    return tuple(jnp.tile(t, (1, rep)) for t in (c, s1, s2))


def _split_w_in(w):
    parts, off = {}, 0
    for name, width in _IN_SPLITS:
        parts[name] = w[:, off:off + width]
        off += width
    cat = lambda names, total: jnp.pad(
        jnp.concatenate([parts[n] for n in names], axis=1),
        ((0, 0), (0, total - sum(parts[n].shape[1] for n in names))))
    zeros_ki = jnp.zeros_like(parts['k_i'])
    w64 = jnp.concatenate([parts['q_i'], parts['k_i'], zeros_ki, zeros_ki, parts['k_i']], axis=1)
    assert w64.shape[1] == W_R64
    return tuple(_with_bf16(g) for g in (cat(_GROUP_ROPE128, 3072), w64, cat(_GROUP_PLAIN, W_PLAIN), parts['g_mix']))


def _project(h, wsplit, tabs128, tabs64, precise=False):
    w128, w64, wpl, wg = (w[0] if precise else w[1] for w in wsplit)
    both = ('f32',) if precise else ('f32', 'bf16')
    z128 = _mm(h, w128, 512, 'rope', ROT_DIM // 2, tabs128, out_kinds=both, precise=precise, name='proj_rope128')
    z64 = _mm(h, w64, 640, 'rope', ROT_IDX // 2, tabs64, out_kinds=both + (() if precise else ('lo',)), precise=precise,
              name='proj_rope64')
    zpl = _mm(h, wpl, 640, out_kinds=both, precise=precise, name='proj_plain')
    zg = _mm(h, wg, 512, 'sigmoid', precise=precise, name='proj_gmix')
    return z128, z64, zpl, zg


def _stack_heads(q_ref, kv):
    return jnp.concatenate([q_ref[0, :, (kv * G + g) * DH:(kv * G + g + 1) * DH] for g in range(G)], axis=0).astype(BF16)


def _masked_attend(q, k, v, mask):
    s = _dg_nt(q, k) * SCALE
    p = jnp.concatenate([_mprobs(s[g * Q_BLK:(g + 1) * Q_BLK], mask).astype(BF16) for g in range(G)], axis=0)
    return jnp.dot(p, v, preferred_element_type=F32)


def _dsa_prompt_body(qa_ref, ka_ref, va_ref, qih_ref, qil_ref, kih_ref, kil_ref, ki2h_ref, ki2l_ref, wi_ref, o_ref,
                     *, k_sel, ext):
    i = pl.program_id(1)
    w = wi_ref[0][:, :H_I] * (H_I ** -0.5)
    npair = H_I * D_IDX // 128
    pairs = lambda r: jnp.concatenate([r[0, :, j * 128:(j + 1) * 128] for j in range(npair)], axis=0)
    qh = pairs(qih_ref)
    q3 = jnp.concatenate([qh, qh, pairs(qil_ref)], axis=1)
    parts = []
    for c0 in range(0, ext, KEY_CLS):
        rows = lambda r1, r2: jnp.concatenate([r1[0, c0:c0 + KEY_CLS, :], r2[0, c0:c0 + KEY_CLS, :]], axis=0)
        kh = rows(kih_ref, ki2h_ref)
        d = _dg_nt(q3, jnp.concatenate([kh, rows(kil_ref, ki2l_ref), kh], axis=1)) * (D_IDX ** -0.5)
        sc = None
        for j in range(npair):
            dj = d[j * Q_BLK:(j + 1) * Q_BLK]
            t = (jnp.maximum(dj[:, :KEY_CLS], 0.0) * w[:, 2 * j:2 * j + 1]
                 + jnp.maximum(dj[:, KEY_CLS:], 0.0) * w[:, 2 * j + 1:2 * j + 2])
            sc = t if sc is None else sc + t
        parts.append(sc)
    score = jnp.concatenate(parts, axis=1)
    qpos = i * Q_BLK + lax.broadcasted_iota(I32, (Q_BLK, 1), 0)
    kpos = lax.broadcasted_iota(I32, (1, ext), 1)
    masked = jnp.where(kpos <= qpos, score, NEG)
    key = jnp.where(masked > NEG * 0.5, _order_key(masked), INT_MIN)
    count_ge = lambda c: jnp.sum(jnp.where(key >= c, 1.0, 0.0), axis=1, keepdims=True)
    n_it = jnp.where((i + 1) * Q_BLK > k_sel, 32, 0)
    thr = _kth_largest_key(count_ge, Q_BLK, k_sel, n_it)
    sel = key >= jnp.maximum(thr, INT_MIN + 1)
    for kv in range(KVH_A):
        sl = slice(kv * DH, (kv + 1) * DH)
        o = _masked_attend(_stack_heads(qa_ref, kv), ka_ref[0, :ext, sl], va_ref[0, :ext, sl], sel)
        for g in range(G):
            o_ref[0, :, (kv * G + g) * DH:(kv * G + g + 1) * DH] = o[g * Q_BLK:(g + 1) * Q_BLK]


def _by_key_class(body, S, **kw):
    def kern(*refs):
        i = pl.program_id(1)
        for c in range(S // KEY_CLS):
            pl.when(i // (KEY_CLS // Q_BLK) == c)(functools.partial(body, *refs, ext=(c + 1) * KEY_CLS, **kw))
    return kern


def _dsa_prompt(z128h, z64h, z64l, zplh, zpl, B, S):
    k_sel = min(TOPK_MAX, S // 4)
    blk = lambda w, f: pl.BlockSpec((1, Q_BLK, w), f)
    full = lambda w, c: pl.BlockSpec((1, S, w), lambda b, i: (b, 0, c))
    return pl.pallas_call(
        _by_key_class(_dsa_prompt_body, S, k_sel=k_sel), grid=(B, S // Q_BLK), name='dsa_prompt',
        in_specs=[blk(1024, lambda b, i: (b, i, 0)), full(256, 8), full(256, 0),
                  blk(1024, lambda b, i: (b, i, 0)), blk(1024, lambda b, i: (b, i, 0)),
                  full(128, 8), full(128, 8), full(128, 9), full(128, 9), blk(128, lambda b, i: (b, i, 8))],
        out_specs=blk(1024, lambda b, i: (b, i, 0)),
        out_shape=jax.ShapeDtypeStruct((B, S, H_A * DH), F32),
        compiler_params=_params('parallel', 'parallel'),
    )(z128h, z128h, zplh, z64h, z64l, z64h, z64l, z64h, z64l, zpl)


def _cmp_partial_kernel(*refs, n_src, paged, precise):
    refs = refs[len(refs) - (2 * n_src + 8):]
    rw = KVH_B * DH

    def mm(x, w_ref, p):
        if not precise:
            return _dot(x, w_ref[3 * p * DH:(3 * p + 1) * DH, :])
        xh, xl = _split(x)
        return jnp.dot(jnp.concatenate([xh, xl, xh], axis=1), w_ref[3 * p * DH:3 * (p + 1) * DH, :], preferred_element_type=F32)

    def rows_of(r, p, kvh):
        if paged:
            return r[0, 0, pl.ds(KVH_B * p + kvh, PAGE_SIZE // S_CMP, stride=KVH_B * S_CMP), :]
        return r[0, :, p * rw + kvh * DH:p * rw + (kvh + 1) * DH]
    ksrc, vsrc = refs[:n_src], refs[n_src:2 * n_src]
    pek, w1k, pev, w1v, ak, bk, av, bv = refs[2 * n_src:]
    half = L_CMP // 2
    for srcs, pe_ref, w_ref, a_ref, b_ref in ((ksrc, pek, w1k, ak, bk), (vsrc, pev, w1v, av, bv)):
        for kvh in range(KVH_B):
            acc_a = acc_b = None
            for p in range(half):
                rows = jnp.concatenate([rows_of(r, p, kvh) for r in srcs], axis=0)
                da = mm(rows + pe_ref[p:p + 1, :], w_ref, p)
                db = mm(rows + pe_ref[half + p:half + p + 1, :], w_ref, half + p)
                acc_a = da if acc_a is None else acc_a + da
                acc_b = db if acc_b is None else acc_b + db
            a_ref[0, :, kvh * DH:(kvh + 1) * DH] = acc_a
            b_ref[0, :, kvh * DH:(kvh + 1) * DH] = acc_b


def _cmp_finish_kernel(ak_ref, bk_ref, av_ref, bv_ref, w2k_ref, w2v_ref, kc_ref, vc_ref, *, precise):
    n = ak_ref.shape[1]
    mm = _dot3 if precise else _dot
    for a_ref, b_ref, w2_ref, o_ref in ((ak_ref, bk_ref, w2k_ref, kc_ref), (av_ref, bv_ref, w2v_ref, vc_ref)):
        for kvh in range(KVH_B):
            sl = slice(kvh * DH, (kvh + 1) * DH)
            h = a_ref[0, :, sl] + pltpu.roll(b_ref[0, :, sl], n - 1, 0)
            o_ref[0, :, sl] = mm(jax.nn.gelu(h), w2_ref[...])


def _cmp_finish(parts, w2k, w2v, precise):
    Bt, n, _ = parts[0].shape
    spec = pl.BlockSpec((1, n, KVH_B * DH), lambda b: (b, 0, 0))
    wspec = pl.BlockSpec((DH, DH), lambda b: (0, 0))
    return pl.pallas_call(
        functools.partial(_cmp_finish_kernel, precise=precise), grid=(Bt,), name='cmp_finish',
        in_specs=[spec] * 4 + [wspec] * 2, out_specs=[spec] * 2,
        out_shape=[jax.ShapeDtypeStruct((Bt, n, KVH_B * DH), F32)] * 2,
        compiler_params=_params('parallel'),
    )(*parts, w2k, w2v)


def _cmp_weight_specs(imap):
    return [pl.BlockSpec((L_CMP, DH), imap), pl.BlockSpec((3 * L_CMP * DH, DH), imap)] * 2


def _cmp_w1_slabs(w1):
    hi = w1.astype(BF16)
    lo = (w1 - hi.astype(F32)).astype(BF16)
    r = lambda t: t.reshape(L_CMP, DH, DH)
    return jnp.concatenate([r(hi), r(hi), r(lo)], axis=1).reshape(3 * L_CMP * DH, DH)


def _compress_prompt(z128, zpl, cw, B, S):
    pek, w1k, w2k, pev, w1v, w2v = cw
    nch = S // S_CMP
    const = lambda b: (0, 0)
    out_spec = pl.BlockSpec((1, nch, KVH_B * DH), lambda b: (b, 0, 0))
    src_spec = pl.BlockSpec((1, nch, S_CMP * 256), lambda b: (b, 0, 0))
    chunked = lambda z, c: z[:, :, c:c + 256].reshape(B, nch, S_CMP * 256)
    parts = pl.pallas_call(
        functools.partial(_cmp_partial_kernel, n_src=1, paged=False, precise=False), grid=(B,), name='cmp_partial_prompt',
        in_specs=[src_spec, src_spec] + _cmp_weight_specs(const),
        out_specs=[out_spec] * 4,
        out_shape=[jax.ShapeDtypeStruct((B, nch, KVH_B * DH), F32)] * 4,
        compiler_params=_params('parallel'),
    )(chunked(z128, 2304), chunked(zpl, 256), pek, _cmp_w1_slabs(w1k), pev, _cmp_w1_slabs(w1v))
    return _cmp_finish(parts, w2k, w2v, precise=False)


def _nsa_prompt_body(qb_ref, kc_ref, vc_ref, ks_ref, vs_ref, kw_ref, vw_ref, gt_ref, ov_ref, ex_ref, o_ref,
                     *, n_cmp, n_slc, n_sel, ext):
    i = pl.program_id(1)
    S = ks_ref.shape[1]
    wspan = WINDOW + Q_BLK
    qpos = i * Q_BLK + lax.broadcasted_iota(I32, (Q_BLK, 1), 0)
    lane = lax.broadcasted_iota(I32, (1, 128), 1)
    kpos = lax.broadcasted_iota(I32, (1, ext), 1)
    causal = kpos <= qpos
    vis = (lane * S_CMP + (L_CMP - 1) <= qpos) & (lane < n_cmp)
    jt = qpos // L_SLC
    forced = (lane == 0) | (lane == jt) | (lane == jt - 1)
    wstart = pl.multiple_of(jnp.clip(i * Q_BLK - WINDOW, 0, S - wspan), Q_BLK)
    wpos = wstart + lax.broadcasted_iota(I32, (1, wspan), 1)
    wmask = (wpos <= qpos) & (wpos > qpos - WINDOW)
    gates = jax.nn.sigmoid(gt_ref[0])
    for kv in range(KVH_B):
        sl = slice(kv * DH, (kv + 1) * DH)
        q = _stack_heads(qb_ref, kv)
        s_cmp = _dg_nt(q, kc_ref[0, :, sl].astype(BF16)) * SCALE
        prs = [_mprobs(s_cmp[g * Q_BLK:(g + 1) * Q_BLK], vis) for g in range(G)]
        o_cmp = _dot(jnp.concatenate(prs, axis=0), vc_ref[0, :, sl])
        imp = jnp.dot(jnp.concatenate(_split(prs[0] + prs[1] + prs[2] + prs[3]), axis=1), ov_ref[...],
                      preferred_element_type=F32)
        sc = jnp.where(lane > jt, NEG, jnp.where(forced, FORCED, imp))
        rank = jnp.zeros((Q_BLK, 128), F32)
        for j2 in range(n_slc):
            col = sc[:, j2:j2 + 1]
            rank = rank + jnp.where(col > sc, 1.0, jnp.where((col == sc) & (lane > j2), 1.0, 0.0))
        sel = jnp.where((rank < n_sel) & (sc > NEG * 0.5), 1.0, 0.0).astype(BF16)
        tokmask = jnp.where(causal, jnp.dot(sel, ex_ref[:, :ext], preferred_element_type=F32), 0.0) > 0.5
        o_slc = _masked_attend(q, ks_ref[0, :ext, sl], vs_ref[0, :ext, sl], tokmask)
        o_win = _masked_attend(q, kw_ref[0, pl.ds(wstart, wspan), sl], vw_ref[0, pl.ds(wstart, wspan), sl], wmask)
        for g in range(G):
            hd = kv * G + g
            rows = slice(g * Q_BLK, (g + 1) * Q_BLK)
            c0 = GATE_OFF + hd * 3
            o_ref[0, :, hd * DH:(hd + 1) * DH] = (gates[:, c0:c0 + 1] * o_cmp[rows] + gates[:, c0 + 1:c0 + 2] * o_slc[rows]
                                                  + gates[:, c0 + 2:c0 + 3] * o_win[rows])


def _overlap_np(nc, nslc, rows, cols):
    cs = np.arange(rows)[:, None] * S_CMP
    js = np.arange(cols)[None, :] * L_SLC
    ov = (cs < js + L_SLC) & (cs + L_CMP > js) & (np.arange(rows)[:, None] < nc) & (np.arange(cols)[None, :] < nslc)
    return jnp.asarray(ov.astype(np.float32), dtype=BF16)


def _nsa_prompt(z128h, zplh, zpl, kc, vc, B, S):
    n_cmp = S // S_CMP - L_CMP // S_CMP + 1
    n_slc = -(-S // L_SLC)
    assert n_cmp <= 128 and n_slc <= 128 and kc.shape[1] == 128
    ov = _overlap_np(n_cmp, n_slc, 128, 128)
    ov = jnp.concatenate([ov, ov], axis=0)
    ex = jnp.asarray((np.arange(128)[:, None] == np.arange(S)[None, :] // L_SLC).astype(np.float32), dtype=BF16)
    blk = lambda w, f: pl.BlockSpec((1, Q_BLK, w), f)
    full = lambda w, c: pl.BlockSpec((1, S, w), lambda b, i: (b, 0, c))
    const = lambda shape: pl.BlockSpec(shape, lambda b, i: (0, 0))
    cspec = pl.BlockSpec((1, 128, 256), lambda b, i: (b, 0, 0))
    return pl.pallas_call(
        _by_key_class(_nsa_prompt_body, S, n_cmp=n_cmp, n_slc=n_slc, n_sel=min(N_SEL_MAX, n_slc)),
        grid=(B, S // Q_BLK), name='nsa_prompt',
        in_specs=[blk(1024, lambda b, i: (b, i, 1)), cspec, cspec, full(256, 10), full(256, 2), full(256, 11), full(256, 3),
                  blk(128, lambda b, i: (b, i, 8)), const((256, 128)), const((128, S))],
        out_specs=blk(1024, lambda b, i: (b, i, 0)),
        out_shape=jax.ShapeDtypeStruct((B, S, H_B * DH), F32),
        compiler_params=_params('parallel', 'parallel'),
    )(z128h, kc, vc, z128h, zplh, z128h, zplh, zpl, ov, ex)


def _merge_u_kernel(*refs, precise):
    oa_ref, ob_ref, wa_ref, wb_ref, ga_ref, gb_ref, o_ref = refs
    mm = _dot3 if precise else _dot
    a, b = mm(oa_ref[...], wa_ref[...]), mm(ob_ref[...], wb_ref[...])
    o_ref[...] = (ga_ref[...] * a + gb_ref[...] * b).astype(o_ref.dtype)


def _merge(x, o_a, o_b, zg, wa, wb, wo, precise=False):
    M = x.shape[0]
    tm, tn = min(M, 512), 512
    nj = D_MODEL // tn
    wspec = lambda k: pl.BlockSpec((k, tn), lambda i, j: (0, j))
    ws = (wa[0], wb[0]) if precise else (wa[1], wb[1])
    wspecs = [wspec(H_A * DH), wspec(H_B * DH)]
    u = pl.pallas_call(
        functools.partial(_merge_u_kernel, precise=precise), grid=(M // tm, nj), name='merge_u',
        in_specs=[pl.BlockSpec((tm, H_A * DH), lambda i, j: (i, 0)), pl.BlockSpec((tm, H_B * DH), lambda i, j: (i, 0))] + wspecs
        + [pl.BlockSpec((tm, tn), lambda i, j: (i, j)), pl.BlockSpec((tm, tn), lambda i, j: (i, j + nj))],
        out_specs=pl.BlockSpec((tm, tn), lambda i, j: (i, j)),
        out_shape=jax.ShapeDtypeStruct((M, D_MODEL), F32 if precise else BF16),
        compiler_params=_params('parallel', 'parallel'),
    )(o_a, o_b, *ws, zg, zg)
    return _mm(u, wo[0] if precise else wo[1], 512, 'res', extra=(x,), precise=precise, name='merge_out')


def _ffn_kernel(te_ref, nu_ref, tr_ref, h_ref, *rest, has_res, n_var, precise):
    nw = 3
    w = rest[:nw]
    o_ref, acc_ref = rest[-2], rest[-1]
    m, f = pl.program_id(0), pl.program_id(1)
    tm = acc_ref.shape[0]
    valid = tr_ref[m]

    @pl.when(f == 0)
    def _():
        acc_ref[...] = jnp.zeros_like(acc_ref)

    def part(rows):
        def run():
            h = h_ref[:rows, :]
            if precise:
                a = _dot3(h, w[0][0])
                b = _dot3(h, w[1][0])
                acc_ref[:rows, :] += _dot3(a * jax.nn.sigmoid(a) * b, w[2][0])
            else:
                a = jnp.dot(h, w[0][0], preferred_element_type=F32)
                b = jnp.dot(h, w[1][0], preferred_element_type=F32)
                acc_ref[:rows, :] += _dot(a * jax.nn.sigmoid(a) * b, w[2][0])
        return run

    step = tm // n_var
    for c in range(n_var):
        pl.when((valid > c * step) & (valid <= (c + 1) * step))(part((c + 1) * step))

    @pl.when(f == pl.num_programs(1) - 1)
    def _():
        o_ref[...] = acc_ref[...] + rest[nw][...] if has_res else acc_ref[...]


def _ffn(h, wg, wu, wd, tile_expert, n_used, tile_rows, res=None, precise=False):
    R, D = h.shape
    F = wg.shape[2]
    tm, tf = min(R, FFN_TM), 512
    nf = F // tf
    fe = lambda m, f, nu: jnp.where(m < nu[0], f, nf - 1)
    up_spec = pl.BlockSpec((1, D, tf), lambda m, f, te, nu, tr: (te[m], 0, fe(m, f, nu)))
    down_spec = pl.BlockSpec((1, tf, D), lambda m, f, te, nu, tr: (te[m], fe(m, f, nu), 0))
    in_specs = [pl.BlockSpec((tm, D), lambda m, f, te, nu, tr: (m, 0)), up_spec, up_spec, down_spec]
    args = [h, wg, wu, wd]
    if res is not None:
        in_specs.append(pl.BlockSpec((tm, D), lambda m, f, te, nu, tr: (m, 0)))
        args.append(res)
    return pl.pallas_call(
        functools.partial(_ffn_kernel, has_res=res is not None, n_var=4 if tm == FFN_TM else 1, precise=precise), name='ffn',
        grid_spec=pltpu.PrefetchScalarGridSpec(
            num_scalar_prefetch=3, grid=(R // tm, nf), in_specs=in_specs,
            out_specs=pl.BlockSpec((tm, D), lambda m, f, te, nu, tr: (m, 0)),
            scratch_shapes=[pltpu.VMEM((tm, D), F32)]),
        out_shape=jax.ShapeDtypeStruct((R, D), F32),
        compiler_params=_params('parallel', 'arbitrary'),
    )(tile_expert, n_used, tile_rows, *args)


def _dense_ffn(x, g, wg, wu, wd, precise=False):
    h = _rms(x, g, F32 if precise else BF16)
    tm = min(x.shape[0], FFN_TM)
    n_tiles = x.shape[0] // tm
    pick = lambda w: w[0 if precise else 1][None]
    return _ffn(h, pick(wg), pick(wu), pick(wd), jnp.zeros((n_tiles,), I32), jnp.full((1,), n_tiles, I32),
                jnp.full((n_tiles,), tm, I32), res=x, precise=precise)


def _router_kernel(x_ref, g_ref, w_ref, h_ref, idx_ref, gate_ref):
    x = x_ref[...]
    h = x * lax.rsqrt(jnp.mean(x * x, axis=-1, keepdims=True) + EPS) * g_ref[...]
    h_ref[...] = h.astype(BF16)
    logits = _dot3(h, w_ref[...])
    lane = lax.broadcasted_iota(I32, logits.shape, 1)
    lanef = lane.astype(F32)
    l1 = jnp.where(lane < N_EXP, logits, -jnp.inf)
    m1 = jnp.max(l1, axis=1, keepdims=True)
    i1 = jnp.min(jnp.where(l1 == m1, lanef, 128.0), axis=1, keepdims=True)
    l2 = jnp.where(lanef == i1, -jnp.inf, l1)
    m2 = jnp.max(l2, axis=1, keepdims=True)
    i2 = jnp.min(jnp.where(l2 == m2, lanef, 128.0), axis=1, keepdims=True)
    e = jnp.exp(m2 - m1)
    inv = 1.0 / (1.0 + e)
    idx_ref[...] = jnp.where(lane == 0, i1, i2).astype(I32)
    gate_ref[...] = jnp.where(lane == 0, inv, e * inv)


def _router(x, g, w_r):
    M, D = x.shape
    tm = min(M, 512)
    wpad = jnp.pad(w_r, ((0, 0), (0, 128 - N_EXP)))
    row = lambda w, dt: (pl.BlockSpec((tm, w), lambda i: (i, 0)), jax.ShapeDtypeStruct((M, w), dt))
    outs = [row(D, BF16), row(128, I32), row(128, F32)]
    h, idx, gate = pl.pallas_call(
        _router_kernel, grid=(M // tm,), name='router',
        in_specs=[pl.BlockSpec((tm, D), lambda i: (i, 0)), pl.BlockSpec((1, D), lambda i: (0, 0)),
                  pl.BlockSpec((D, 128), lambda i: (0, 0))],
        out_specs=[o[0] for o in outs], out_shape=[o[1] for o in outs],
        compiler_params=_params('parallel'),
    )(x, g.reshape(1, D), wpad)
    return h, idx[:, :2], gate[:, :2]


def _moe(xp, xs, g, w_r, wg, wu, wd):
    hp, ip, gp = _router(xp, g, w_r)
    hs, is_, gs = _router(xs, g, w_r)
    h = jnp.concatenate([hp, hs], axis=0)
    idx = jnp.concatenate([ip, is_], axis=0)
    gate = jnp.concatenate([gp, gs], axis=0)
    M = h.shape[0]
    tm = FFN_TM
    n_tiles = (2 * M + N_EXP * (tm - 1) + tm - 1) // tm
    e = jnp.concatenate([idx[:, 0], idx[:, 1]])
    onehot = (e[:, None] == jnp.arange(N_EXP, dtype=I32)[None, :]).astype(I32)
    rank = jnp.sum((jnp.cumsum(onehot, axis=0) - onehot) * onehot, axis=1)
    counts = jnp.sum(onehot, axis=0)
    tiles_per = (counts + tm - 1) // tm
    tile_end = jnp.cumsum(tiles_per)
    row_start = (tile_end - tiles_per) * tm
    pos = jnp.sum(onehot * row_start[None, :], axis=1) + rank
    src_tok = jnp.zeros((n_tiles * tm,), I32).at[pos].set(jnp.arange(2 * M, dtype=I32) % M)
    n_used = tile_end[-1].astype(I32)
    tid = jnp.arange(n_tiles, dtype=I32)
    t = jnp.minimum(tid, n_used - 1)
    tile_expert = jnp.sum((t[:, None] >= tile_end[None, :]).astype(I32), axis=1)
    rows_left = counts[tile_expert] - (t - (tile_end - tiles_per)[tile_expert]) * tm
    tile_rows = jnp.where(tid < n_used, jnp.clip(rows_left, 0, tm), 0).astype(I32)
    y = _ffn(jnp.take(h, src_tok, axis=0), wg, wu, wd, tile_expert, n_used.reshape(1), tile_rows)
    f = gate[:, 0:1] * jnp.take(y, pos[:M], axis=0) + gate[:, 1:2] * jnp.take(y, pos[M:], axis=0)
    Mp = xp.shape[0]
    return xp + f[:Mp], xs + f[Mp:]


def _page_specs(block, n_lead, layer, col=None):
    def spec(j):
        def imap(b, s, pt):
            page = pt[b * (PAST_LEN // PAGE_SIZE) + s * PAGES_PER_STEP + j]
            return (layer, page) + (0,) * n_lead
        return pl.BlockSpec(block, imap)
    return [spec(j) for j in range(PAGES_PER_STEP)]


def _idx_walk_kernel(pt_ref, q_ref, w_ref, *refs):
    pages, o_ref = refs[:-1], refs[-1]
    keys = jnp.concatenate([r[0, 0] for r in pages], axis=0)
    d = _dot3_nt(q_ref[0], keys) * (D_IDX ** -0.5)
    o_ref[0, 0] = jnp.sum(jnp.maximum(d, 0.0) * (w_ref[0] * (H_I ** -0.5)), axis=0, keepdims=True)


def _dsa_thresh_kernel(sc_ref, qi_ref, wi_ref, kin_ref, sel_ref, selnew_ref, *, k_sel):
    nb = sc_ref.shape[0]
    snew = []
    for b in range(nb):
        d = jnp.sum(qi_ref[b] * kin_ref[b:b + 1, :], axis=1, keepdims=True) * (D_IDX ** -0.5)
        snew.append(jnp.sum(jnp.maximum(d, 0.0) * (wi_ref[b] * (H_I ** -0.5)), axis=0, keepdims=True))
    snew = jnp.concatenate(snew, axis=0)
    key, knew = _order_key(sc_ref[...]), _order_key(snew)
    count_ge = lambda c: (jnp.sum(jnp.where(key >= c, 1.0, 0.0), axis=1, keepdims=True) + jnp.where(knew >= c, 1.0, 0.0))
    thr = _kth_largest_key(count_ge, nb, k_sel)
    sel_ref[...] = jnp.where((key >= thr) & (sc_ref[...] > NEG * 0.5), 1.0, 0.0)
    selnew_ref[...] = jnp.broadcast_to(jnp.where((knew >= thr) & (snew > NEG * 0.5), 1.0, 0.0), selnew_ref.shape)


def _dsa_walk_kernel(pt_ref, q_ref, sel_ref, selnew_ref, knew_ref, vnew_ref, *refs):
    n = PAGES_PER_STEP
    kpages, vpages = refs[:n], refs[n:2 * n]
    o_ref, m_ref, l_ref, acc_ref = refs[2 * n:]
    s_id = pl.program_id(1)
    row = lax.broadcasted_iota(I32, (H_A, 1), 0)
    first = row < G

    @pl.when(s_id == 0)
    def _():
        m_ref[...] = jnp.full_like(m_ref, NEG)
        l_ref[...] = jnp.zeros_like(l_ref)
        acc_ref[...] = jnp.zeros_like(acc_ref)

    q = q_ref[0]
    k = jnp.concatenate([r[0, 0] for r in kpages], axis=0).astype(BF16)
    v = jnp.concatenate([r[0, 0] for r in vpages], axis=0).astype(BF16)
    col = lax.broadcasted_iota(I32, (1, k.shape[0]), 1)
    mask = jnp.where((col & 1) == jnp.where(first, 0, 1), sel_ref[0, 0], 0.0) > 0.5
    s = jnp.where(mask, _dot_nt(q, k) * SCALE, NEG)
    m_new = jnp.maximum(m_ref[...], jnp.max(s, axis=1, keepdims=True))
    p = jnp.where(mask, jnp.exp(s - m_new), 0.0)
    alpha = jnp.exp(m_ref[...] - m_new)
    l_ref[...] = alpha * l_ref[...] + jnp.sum(p, axis=1, keepdims=True)
    acc_ref[...] = alpha * acc_ref[...] + jnp.dot(p.astype(BF16), v, preferred_element_type=F32)
    m_ref[...] = m_new

    @pl.when(s_id == pl.num_programs(1) - 1)
    def _():
        kn = jnp.where(first, knew_ref[0][:, :DH], knew_ref[0][:, DH:])
        vn = jnp.where(first, vnew_ref[0][:, :DH], vnew_ref[0][:, DH:])
        valid = selnew_ref[0][:, :1] > 0.5
        sn = jnp.where(valid, jnp.sum(q * kn, axis=1, keepdims=True) * SCALE, NEG)
        m_fin = jnp.maximum(m_ref[...], sn)
        pn = jnp.where(valid, jnp.exp(sn - m_fin), 0.0)
        a2 = jnp.exp(m_ref[...] - m_fin)
        l_fin = a2 * l_ref[...] + pn
        o_ref[0] = (a2 * acc_ref[...] + pn * vn) * (1.0 / jnp.maximum(l_fin, 1e-30))


def _dsa_sample(l, z128, z64, zpl, cache_k, cache_v, cache_ki, pt_flat, DB):
    n_pages = PAST_LEN // PAGE_SIZE
    n_steps = n_pages // PAGES_PER_STEP
    span = PAGES_PER_STEP * PAGE_SIZE
    k_sel = min(TOPK_MAX, (PAST_LEN + 1) // 4)
    qi = z64[:DB, :H_I * D_IDX].reshape(DB, H_I, D_IDX)
    wi = zpl[:DB, 1024:1024 + H_I].reshape(DB, H_I, 1)
    kin = z64[:DB, 1024:1024 + D_IDX]
    scores = pl.pallas_call(
        _idx_walk_kernel, name='idx_walk',
        grid_spec=pltpu.PrefetchScalarGridSpec(
            num_scalar_prefetch=1, grid=(DB, n_steps),
            in_specs=[pl.BlockSpec((1, H_I, D_IDX), lambda b, s, pt: (b, 0, 0)), pl.BlockSpec((1, H_I, 1), lambda b, s, pt: (b, 0, 0))]
            + _page_specs((1, 1, PAGE_SIZE, D_IDX), 2, l),
            out_specs=pl.BlockSpec((1, 1, 1, span), lambda b, s, pt: (b, s, 0, 0))),
        out_shape=jax.ShapeDtypeStruct((DB, n_steps, 1, span), F32),
        compiler_params=_params('parallel', 'parallel'),
    )(pt_flat, qi, wi, *([cache_ki] * PAGES_PER_STEP))
    sel, selnew = pl.pallas_call(
        functools.partial(_dsa_thresh_kernel, k_sel=k_sel), name='dsa_thresh',
        out_shape=[jax.ShapeDtypeStruct((DB, PAST_LEN), F32), jax.ShapeDtypeStruct((DB, 128), F32)],
        compiler_params=pltpu.CompilerParams(vmem_limit_bytes=VMEM_LIMIT),
    )(scores.reshape(DB, PAST_LEN), qi, wi, kin)
    qa = z128[:DB, :H_A * DH].reshape(DB, H_A, DH)
    knew = jnp.broadcast_to(z128[:DB, None, 2048:2304], (DB, H_A, KVH_A * DH))
    vnew = jnp.broadcast_to(zpl[:DB, None, 0:256], (DB, H_A, KVH_A * DH))
    selnew = jnp.broadcast_to(selnew[:, None, :], (DB, H_A, 128))
    per_b = lambda shape: pl.BlockSpec(shape, lambda b, s, pt: (b, 0, 0))
    o = pl.pallas_call(
        _dsa_walk_kernel, name='dsa_walk',
        grid_spec=pltpu.PrefetchScalarGridSpec(
            num_scalar_prefetch=1, grid=(DB, n_steps),
            in_specs=[per_b((1, H_A, DH)), pl.BlockSpec((1, 1, 1, KVH_A * span), lambda b, s, pt: (b, s, 0, 0)),
                      per_b((1, H_A, 128)), per_b((1, H_A, KVH_A * DH)), per_b((1, H_A, KVH_A * DH))]
            + _page_specs((1, 1, KVH_A * PAGE_SIZE, DH), 2, l) * 2,
            out_specs=per_b((1, H_A, DH)),
            scratch_shapes=[pltpu.VMEM((H_A, 1), F32), pltpu.VMEM((H_A, 1), F32), pltpu.VMEM((H_A, DH), F32)]),
        out_shape=jax.ShapeDtypeStruct((DB, H_A, DH), F32),
        compiler_params=_params('parallel', 'arbitrary'),
    )(pt_flat, qa, jnp.repeat(sel, KVH_A, axis=1).reshape(DB, n_steps, 1, KVH_A * span), selnew, knew, vnew,
      *([cache_k] * PAGES_PER_STEP), *([cache_v] * PAGES_PER_STEP))
    return o.reshape(DB, H_A * DH)


def _compress_sample(l, cache_k, cache_v, cw, pt_flat, DB):
    pek, w1k, w2k, pev, w1v, w2v = cw
    n_pages = PAST_LEN // PAGE_SIZE
    n_steps = n_pages // PAGES_PER_STEP
    cps = PAGE_SIZE // S_CMP
    nch_step = PAGES_PER_STEP * cps
    const = lambda b, s, pt: (0, 0)
    out_spec = pl.BlockSpec((1, nch_step, KVH_B * DH), lambda b, s, pt: (b, s, 0))
    page_specs = _page_specs((1, 1, KVH_B * PAGE_SIZE, DH), 2, l)
    parts = pl.pallas_call(
        functools.partial(_cmp_partial_kernel, n_src=PAGES_PER_STEP, paged=True, precise=False), name='cmp_partial_sample',
        grid_spec=pltpu.PrefetchScalarGridSpec(
            num_scalar_prefetch=1, grid=(DB, n_steps),
            in_specs=page_specs + page_specs + _cmp_weight_specs(const),
            out_specs=[out_spec] * 4),
        out_shape=[jax.ShapeDtypeStruct((DB, PAST_LEN // S_CMP, KVH_B * DH), F32)] * 4,
        compiler_params=_params('parallel', 'parallel'),
    )(pt_flat, *([cache_k] * PAGES_PER_STEP), *([cache_v] * PAGES_PER_STEP), pek, _cmp_w1_slabs(w1k), pev, _cmp_w1_slabs(w1v))
    return _cmp_finish(parts, w2k, w2v, precise=True)


def _cmp_select_sample_kernel(q_ref, kc_ref, vc_ref, ov_ref, o_ref, ids_ref, *, n_cmp, n_slc, n_sel):
    n = kc_ref.shape[1]
    q = q_ref[0]
    row = lax.broadcasted_iota(I32, (H_B, 1), 0)
    first = row < G
    kc = kc_ref[0]
    vc = vc_ref[0]
    s = jnp.where(first, _dot3_nt(q, kc[:, :DH]), _dot3_nt(q, kc[:, DH:])) * SCALE
    col = lax.broadcasted_iota(I32, (1, n), 1)
    qpos = PAST_LEN
    pr = _mprobs(s, (col * S_CMP + (L_CMP - 1) <= qpos) & (col < n_cmp))
    o_ref[0] = jnp.where(first, _dot3(pr, vc[:, :DH]), _dot3(pr, vc[:, DH:]))
    psum = jnp.concatenate([jnp.sum(jnp.where((row >= kv * G) & (row < (kv + 1) * G), pr, 0.0), axis=0, keepdims=True)
                            for kv in range(KVH_B)] + [jnp.zeros((8 - KVH_B, n), F32)], axis=0)
    ph, plo = _split(psum)
    imp = jnp.dot(ph, ov_ref[...], preferred_element_type=F32) + jnp.dot(plo, ov_ref[...], preferred_element_type=F32)
    j = lax.broadcasted_iota(I32, imp.shape, 1)
    jf = j.astype(F32)
    jt = qpos // L_SLC
    forced = (j == 0) | (j == jt) | (j == jt - 1)
    sc = jnp.where((j > jt) | (j >= n_slc), NEG, jnp.where(forced, FORCED, imp))
    lane = lax.broadcasted_iota(I32, (8, 128), 1)
    ids = jnp.zeros((8, 128), F32)
    for r in range(n_sel):
        m = jnp.max(sc, axis=1, keepdims=True)
        pick = jnp.min(jnp.where(sc == m, jf, 1e9), axis=1, keepdims=True)
        ids = jnp.where(lane == r, pick, ids)
        sc = jnp.where(jf == pick, -jnp.inf, sc)
    ids_ref[0] = ids.astype(I32)


def _slc_sample_kernel(ids_ref, pt_ref, q_ref, knew_ref, vnew_ref, *refs, n_sel, n_past):
    kblks, vblks, o_ref = refs[:n_sel], refs[n_sel:2 * n_sel], refs[2 * n_sel]
    b, kv = pl.program_id(0), pl.program_id(1)
    base = (b * KVH_B + kv) * n_sel
    nrow = KVH_B * L_SLC
    row = lax.broadcasted_iota(I32, (nrow, 1), 0)
    lane = lax.broadcasted_iota(I32, (1, n_sel * nrow), 1)
    tpos = (lane % nrow) // KVH_B
    ks, vs = [], []
    for r in range(n_sel):
        bid = ids_ref[base + r]
        ks.append(jnp.where(bid < n_past, kblks[r][0, 0, 0], jnp.where(row == kv, knew_ref[0, 0], 0.0)))
        vs.append(jnp.where(bid < n_past, vblks[r][0, 0, 0], jnp.where(row == kv, vnew_ref[0, 0], 0.0)))
        tpos = tpos + jnp.where(lane // nrow == r, bid * L_SLC, 0)
    mask = jnp.where(lane % KVH_B == kv, tpos, PAST_LEN + 1) <= PAST_LEN
    s = _dot3_nt(q_ref[0, 0], jnp.concatenate(ks, axis=0)) * SCALE
    o_ref[0, 0] = _dot3(_mprobs(s, mask), jnp.concatenate(vs, axis=0))


def _win_sample_kernel(q_ref, kb_ref, vb_ref, kn_ref, vn_ref, o_ref, ko_ref, vo_ref):
    W = kb_ref.shape[2]
    q = q_ref[0]
    rowq = lax.broadcasted_iota(I32, (H_B, 1), 0)
    first = rowq < G
    kb, vb = kb_ref[0, 0], vb_ref[0, 0]
    kn, vn = kn_ref[0], vn_ref[0]
    s = jnp.where(first, _dot3_nt(q, kb[:, :DH]), _dot3_nt(q, kb[:, DH:])) * SCALE
    col = lax.broadcasted_iota(I32, (1, W), 1)
    mask = (PAST_LEN - W + col) > (PAST_LEN - WINDOW)
    knh = jnp.where(first, kn[:, :DH], kn[:, DH:])
    vnh = jnp.where(first, vn[:, :DH], vn[:, DH:])
    sn = jnp.sum(q * knh, axis=1, keepdims=True) * SCALE
    s = jnp.where(mask, s, NEG)
    m = jnp.maximum(jnp.max(s, axis=1, keepdims=True), sn)
    p = jnp.where(mask, jnp.exp(s - m), 0.0)
    pn = jnp.exp(sn - m)
    inv = 1.0 / jnp.maximum(jnp.sum(p, axis=1, keepdims=True) + pn, 1e-30)
    pn_ = p * inv
    pv = jnp.where(first, _dot3(pn_, vb[:, :DH]), _dot3(pn_, vb[:, DH:]))
    o_ref[0] = pv + (pn * inv) * vnh
    roww = lax.broadcasted_iota(I32, (W, 1), 0)
    ko_ref[0] = jnp.where(roww == W - 1, kn[:1, :], pltpu.roll(kb, W - 1, 0))
    vo_ref[0] = jnp.where(roww == W - 1, vn[:1, :], pltpu.roll(vb, W - 1, 0))


def _nsa_sample(l, z128, zpl, cache_ck, cache_cv, cache_sk, cache_sv, buf_k, buf_v, cw, pt_flat, DB):
    kc, vc = _compress_sample(l, cache_ck, cache_cv, cw, pt_flat, DB)
    n = kc.shape[1]
    L = PAST_LEN + 1
    n_cmp = L // S_CMP - L_CMP // S_CMP + 1
    n_slc = -(-L // L_SLC)
    n_sel = min(N_SEL_MAX, n_slc)
    n_past = PAST_LEN // L_SLC
    cols = -(-n_slc // 128) * 128
    qb = z128[:DB, 1024:2048].reshape(DB, H_B, DH)
    per_b = lambda shape: pl.BlockSpec(shape, lambda b: (b,) + (0,) * (len(shape) - 1))
    o_cmp, ids = pl.pallas_call(
        functools.partial(_cmp_select_sample_kernel, n_cmp=n_cmp, n_slc=n_slc, n_sel=n_sel), grid=(DB,), name='cmp_select_sample',
        in_specs=[per_b((1, H_B, DH)), per_b((1, n, 256)), per_b((1, n, 256)), pl.BlockSpec((n, cols), lambda b: (0, 0))],
        out_specs=[per_b((1, H_B, DH)), per_b((1, 8, 128))],
        out_shape=[jax.ShapeDtypeStruct((DB, H_B, DH), F32), jax.ShapeDtypeStruct((DB, 8, 128), I32)],
        compiler_params=_params('parallel'),
    )(qb, kc, vc, _overlap_np(n_cmp, n_slc, n, cols))
    ids_flat = ids[:, :KVH_B, :n_sel].reshape(-1)
    q4 = jnp.pad(qb.reshape(DB, KVH_B, G, DH), ((0, 0), (0, 0), (0, 8 - G), (0, 0)))
    knew = z128[:DB, 2560:2816].reshape(DB, KVH_B, 1, DH)
    vnew = zpl[:DB, 512:768].reshape(DB, KVH_B, 1, DH)
    bpp = PAGE_SIZE // L_SLC
    n_pages = PAST_LEN // PAGE_SIZE

    nrow = KVH_B * L_SLC

    def blk_spec(r):
        def imap(b, kv, ids_s, pt):
            bid = jnp.minimum(ids_s[(b * KVH_B + kv) * n_sel + r], n_past - 1)
            return (l, pt[b * n_pages + bid // bpp], bid % bpp, 0, 0)
        return pl.BlockSpec((1, 1, 1, nrow, DH), imap)

    per_bk = lambda shape: pl.BlockSpec(shape, lambda b, kv, ids_s, pt: (b, kv, 0, 0))
    view = lambda c: c.reshape(c.shape[0], c.shape[1], bpp, nrow, DH)
    blk_specs = [blk_spec(r) for r in range(n_sel)]
    o_slc = pl.pallas_call(
        functools.partial(_slc_sample_kernel, n_sel=n_sel, n_past=n_past), name='slc_sample',
        grid_spec=pltpu.PrefetchScalarGridSpec(
            num_scalar_prefetch=2, grid=(DB, KVH_B),
            in_specs=[per_bk((1, 1, 8, DH)), per_bk((1, 1, 1, DH)), per_bk((1, 1, 1, DH))] + blk_specs + blk_specs,
            out_specs=per_bk((1, 1, 8, DH))),
        out_shape=jax.ShapeDtypeStruct((DB, KVH_B, 8, DH), F32),
        compiler_params=_params('parallel', 'parallel'),
    )(ids_flat, pt_flat, q4, knew, vnew, *([view(cache_sk)] * n_sel), *([view(cache_sv)] * n_sel))
    W = buf_k.shape[2]
    kn8 = jnp.broadcast_to(z128[:DB, None, 2816:3072], (DB, 8, 256))
    vn8 = jnp.broadcast_to(zpl[:DB, None, 768:1024], (DB, 8, 256))
    bufspec = pl.BlockSpec((1, 1, W, 256), lambda b: (l, b, 0, 0))
    o_win, nk, nv = pl.pallas_call(
        _win_sample_kernel, grid=(DB,), name='win_sample',
        in_specs=[per_b((1, H_B, DH)), bufspec, bufspec, per_b((1, 8, 256)), per_b((1, 8, 256))],
        out_specs=[per_b((1, H_B, DH)), per_b((1, W, 256)), per_b((1, W, 256))],
        out_shape=[jax.ShapeDtypeStruct((DB, H_B, DH), F32)] + [jax.ShapeDtypeStruct((DB, W, 256), F32)] * 2,
        compiler_params=_params('parallel'),
    )(qb, buf_k.reshape(buf_k.shape[0], DB, W, 256), buf_v.reshape(buf_v.shape[0], DB, W, 256), kn8, vn8)
    gb = jax.nn.sigmoid(zpl[:DB, 1024 + GATE_OFF:1024 + GATE_OFF + 3 * H_B]).reshape(DB, H_B, 3)
    o_b = gb[..., 0:1] * o_cmp + gb[..., 1:2] * o_slc[:, :, :G].reshape(DB, H_B, DH) + gb[..., 2:3] * o_win
    return o_b.reshape(DB, H_B * DH), nk, nv


def kernel(x_prompt, x_sample, cache_dsa_k, cache_dsa_v, cache_idx_k, cache_cmp_k, cache_cmp_v, cache_slc_k, cache_slc_v, state_win_k, state_win_v, page_table, norm_mix, w_in, cmp_pe_k, cmp_w1_k, cmp_w2_k, cmp_pe_v, cmp_w1_v, cmp_w2_v, w_up_a, w_up_b, w_o, norm_ffn, dense_w_gate, dense_w_up, dense_w_down, moe_w_router, moe_w_gate, moe_w_up, moe_w_down, norm_final):
    B, S, D = x_prompt.shape
    DB, T, _ = x_sample.shape
    depth = w_in.shape[0]
    assert T == 1 and D == D_MODEL and page_table.shape == (DB, PAST_LEN // PAGE_SIZE)
    wp = min(WINDOW, S)
    xp = x_prompt.reshape(B * S, D)
    xs = jnp.pad(x_sample.reshape(DB, D), ((0, SAMPLE_ROWS - DB), (0, 0)))
    pt_flat = page_table.reshape(-1).astype(I32)
    pos_p = jnp.tile(jnp.arange(S), B)
    pos_s = jnp.full((SAMPLE_ROWS,), PAST_LEN)
    tabs_p = (_rope_tables(pos_p, ROT_DIM, DH), _rope_tables(pos_p, ROT_IDX, D_IDX))
    tabs_s = (_rope_tables(pos_s, ROT_DIM, DH), _rope_tables(pos_s, ROT_IDX, D_IDX))
    rows_view = lambda c: c.reshape(c.shape[0], c.shape[1], -1, c.shape[-1])
    c_dsa_k, c_dsa_v, c_idx, c_cmp_k, c_cmp_v, c_slc_k, c_slc_v = map(
        rows_view, (cache_dsa_k, cache_dsa_v, cache_idx_k, cache_cmp_k, cache_cmp_v, cache_slc_k, cache_slc_v))
    newp, news = [], []
    for l in range(depth):
        cw = (cmp_pe_k[l], cmp_w1_k[l], cmp_w2_k[l], cmp_pe_v[l], cmp_w1_v[l], cmp_w2_v[l])
        wsplit = _split_w_in(w_in[l])
        wa, wb, wo = _with_bf16(w_up_a[l]), _with_bf16(w_up_b[l]), _with_bf16(w_o[l])

        z128, z64, zpl, zg = _project(_rms(xp, norm_mix[l], BF16), wsplit, *tabs_p)
        (z128b, z128h), (z64b, z64h, z64l), (zplb, zplh) = (tuple(t.reshape(B, S, -1) for t in z) for z in (z128, z64, zpl))
        o_a = _dsa_prompt(z128h, z64h, z64l, zplh, zplb, B, S)
        kc, vc = _compress_prompt(z128b, zplb, cw, B, S)
        o_b = _nsa_prompt(z128h, zplh, zplb, kc, vc, B, S)
        xp = _merge(xp, o_a.reshape(B * S, -1), o_b.reshape(B * S, -1), zg, wa, wb, wo)
        kv4 = lambda z, c: z[:, :, c:c + 256].reshape(B, S, 2, DH)
        newp.append((kv4(z128b, 2048), kv4(zplb, 0), z64b[:, :, 1024:1024 + D_IDX], kv4(z128b, 2304), kv4(zplb, 256),
                     kv4(z128b, 2560), kv4(zplb, 512), kv4(z128b, 2816)[:, S - wp:], kv4(zplb, 768)[:, S - wp:]))

        y128, y64, ypl, yg = _project(_rms(xs, norm_mix[l], F32), wsplit, *tabs_s, precise=True)
        o_a = _dsa_sample(l, y128, y64, ypl, c_dsa_k, c_dsa_v, c_idx, pt_flat, DB)
        o_b, win_k, win_v = _nsa_sample(l, y128, ypl, c_cmp_k, c_cmp_v, c_slc_k, c_slc_v, state_win_k, state_win_v, cw, pt_flat, DB)
        pad = lambda o: jnp.pad(o, ((0, SAMPLE_ROWS - DB), (0, 0)))
        xs = _merge(xs, pad(o_a), pad(o_b), yg, wa, wb, wo, precise=True)
        s4 = lambda z, c: z[:DB, c:c + 256].reshape(DB, 1, 2, DH)
        news.append((s4(y128, 2048), s4(ypl, 0), y64[:DB, 1024:1024 + D_IDX].reshape(DB, 1, D_IDX), s4(y128, 2304), s4(ypl, 256),
                     s4(y128, 2560), s4(ypl, 512), win_k.reshape(DB, -1, 2, DH), win_v.reshape(DB, -1, 2, DH)))

        i = l // 2
        if l % 2 == 0:
            ws = (_with_bf16(dense_w_gate[i]), _with_bf16(dense_w_up[i]), _with_bf16(dense_w_down[i]))
            xp = _dense_ffn(xp, norm_ffn[l], *ws)
            xs = _dense_ffn(xs, norm_ffn[l], *ws, precise=True)
        else:
            xp, xs = _moe(xp, xs, norm_ffn[l], moe_w_router[i], moe_w_gate[i].astype(BF16), moe_w_up[i].astype(BF16),
                          moe_w_down[i].astype(BF16))

    y_prompt = _rms(xp, norm_final, F32).reshape(B, S, D)
    y_sample = _rms(xs, norm_final, F32)[:DB].reshape(DB, 1, D)
    stack = lambda items: [jnp.stack([it[n] for it in items], axis=0) for n in range(9)]
    return (y_prompt, y_sample, *stack(newp), *stack(news))
```

```python
import functools

import numpy as np
import jax
import jax.numpy as jnp
from jax import lax
from jax.experimental import pallas as pl
from jax.experimental.pallas import tpu as pltpu

F32 = jnp.float32
BF16 = jnp.bfloat16
I32 = jnp.int32

D_MODEL = 2048
PAST_LEN = 16384
PAGE_SIZE = 128
DH = 128
ROT_DIM = DH // 4
ROPE_THETA = 500000.0
H_A = 8
KVH_A = 2
H_I = 16
D_IDX = 64
ROT_IDX = D_IDX // 4
TOPK_MAX = 256
H_B = 8
KVH_B = 2
G = 4
L_CMP = 32
S_CMP = 16
L_SLC = 64
N_SEL_MAX = 16
WINDOW = 512
N_EXP = 8
Q_BLK = 128
FFN_TM = 512
SEARCH_GROUP = 4
KEY_CLS = 512
EPS = 1e-6
NEG = -1e30
FORCED = 1e6
INT_MIN = -2 ** 31
SCALE = DH ** -0.5
PAGES_PER_STEP = 16
SAMPLE_ROWS = 16
VMEM_LIMIT = 48 * 1024 * 1024

_IN_SPLITS = (
    ('q_a', H_A * DH), ('k_a', KVH_A * DH), ('v_a', KVH_A * DH),
    ('q_i', H_I * D_IDX), ('k_i', D_IDX), ('w_i', H_I),
    ('q_b', H_B * DH),
    ('k_cmp', KVH_B * DH), ('v_cmp', KVH_B * DH),
    ('k_slc', KVH_B * DH), ('v_slc', KVH_B * DH),
    ('k_win', KVH_B * DH), ('v_win', KVH_B * DH),
    ('g_b', 3 * H_B), ('g_mix', 2 * D_MODEL),
)
_GROUP_ROPE128 = ('q_a', 'q_b', 'k_a', 'k_cmp', 'k_slc', 'k_win')
_GROUP_ROPE64 = ('q_i', 'k_i')
_GROUP_PLAIN = ('v_a', 'v_cmp', 'v_slc', 'v_win', 'w_i', 'g_b')
W_R64 = 1280
W_PLAIN = 1280
GATE_OFF = H_I


def _params(*sem):
    return pltpu.CompilerParams(dimension_semantics=sem, vmem_limit_bytes=VMEM_LIMIT)


def _dot(a, b):
    return jnp.dot(a.astype(BF16), b.astype(BF16), preferred_element_type=F32)


def _dg_nt(a, b):
    return lax.dot_general(a, b, (((1,), (1,)), ((), ())), preferred_element_type=F32)


def _dot_nt(a, b):
    return _dg_nt(a.astype(BF16), b.astype(BF16))


def _split(a):
    hi = a.astype(BF16)
    lo = (a - hi.astype(F32)).astype(BF16)
    return hi, lo


def _dot3_nt(a, b):
    ah, al = _split(a)
    bh, bl = _split(b)
    return _dg_nt(jnp.concatenate([ah, ah, al], axis=1), jnp.concatenate([bh, bl, bh], axis=1))


def _dot3(a, b):
    ah, al = _split(a)
    bh, bl = _split(b)
    return jnp.dot(jnp.concatenate([ah, ah, al], axis=1), jnp.concatenate([bh, bl, bh], axis=0), preferred_element_type=F32)


def _with_bf16(w):
    return w, w.astype(BF16)


def _mprobs(s, mask):
    s = jnp.where(mask, s, NEG)
    m = jnp.max(s, axis=-1, keepdims=True)
    p = jnp.where(mask, jnp.exp(s - m), 0.0)
    return p * (1.0 / jnp.maximum(jnp.sum(p, axis=-1, keepdims=True), 1e-30))


def _order_key(x):
    b = lax.bitcast_convert_type(x + 0.0, I32)
    return jnp.where(b < 0, b ^ jnp.int32(0x7FFFFFFF), b)


def _kth_largest_key(count_ge, rows, k, n_bits=32):
    lo0 = jnp.full((rows, 1), INT_MIN, I32)

    def cond(c):
        return (c[0] < n_bits) & (c[3] == 0)

    def body(c):
        it, lo, cur, _ = c
        for j in range(SEARCH_GROUP):
            cand = lo + jnp.left_shift(jnp.int32(1), 31 - (it + j))
            cnt = count_ge(cand)
            take = cnt >= k
            lo = jnp.where(take, cand, lo)
            cur = jnp.where(take, cnt, cur)
        done = (jnp.max(jnp.abs(cur - k)) == 0.0).astype(I32)
        return it + SEARCH_GROUP, lo, cur, done

    return lax.while_loop(cond, body, (jnp.int32(0), lo0, count_ge(lo0), jnp.int32(0)))[1]


def _rms_kernel(x_ref, g_ref, o_ref):
    x = x_ref[...]
    y = x * lax.rsqrt(jnp.mean(x * x, axis=-1, keepdims=True) + EPS)
    o_ref[...] = (y * g_ref[...]).astype(o_ref.dtype)


def _rms(x, g, out_dtype):
    M, D = x.shape
    tm = min(M, 512)
    return pl.pallas_call(
        _rms_kernel, grid=(M // tm,), name='rms',
        in_specs=[pl.BlockSpec((tm, D), lambda i: (i, 0)), pl.BlockSpec((1, D), lambda i: (0, 0))],
        out_specs=pl.BlockSpec((tm, D), lambda i: (i, 0)),
        out_shape=jax.ShapeDtypeStruct((M, D), out_dtype),
        compiler_params=_params('parallel'),
    )(x, g.reshape(1, D))


def _mm_kernel(*refs, mode, shift, out_kinds, precise):
    if precise:
        acc = _dot3(refs[0][...], refs[1][...])
    else:
        acc = jnp.dot(refs[0][...], refs[1][...], preferred_element_type=F32)
    if mode == 'rope':
        c_ref, s1_ref, s2_ref = refs[2:5]
        tn = acc.shape[1]
        rep = tn // 128
        til = lambda r: jnp.concatenate([r[...]] * rep, axis=1)
        acc = (acc * til(c_ref) + pltpu.roll(acc, tn - shift, 1) * til(s1_ref)
               + pltpu.roll(acc, shift, 1) * til(s2_ref))
    elif mode == 'sigmoid':
        acc = jax.nn.sigmoid(acc)
    elif mode == 'res':
        acc = acc + refs[2][...]
    for kind, o_ref in zip(out_kinds, refs[len(refs) - len(out_kinds):]):
        if kind == 'lo':
            o_ref[...] = (acc - acc.astype(BF16).astype(F32)).astype(BF16)
        else:
            o_ref[...] = acc.astype(o_ref.dtype)


_KIND_DTYPE = {'f32': F32, 'bf16': BF16, 'lo': BF16}


def _mm(a, b, tn, mode='none', shift=0, extra=(), out_kinds=('f32',), precise=False, name='mm'):
    out_dtypes = tuple(_KIND_DTYPE[k] for k in out_kinds)
    M, K = a.shape
    N = b.shape[1]
    tm = min(M, 1024)
    in_specs = [pl.BlockSpec((tm, K), lambda i, j: (i, 0)), pl.BlockSpec((K, tn), lambda i, j: (0, j))]
    if mode == 'rope':
        in_specs += [pl.BlockSpec((tm, 128), lambda i, j: (i, 0))] * 3
    elif mode == 'res':
        in_specs += [pl.BlockSpec((tm, tn), lambda i, j: (i, j))]
    outs = pl.pallas_call(
        functools.partial(_mm_kernel, mode=mode, shift=shift, out_kinds=out_kinds, precise=precise),
        grid=(M // tm, N // tn), name=name,
        in_specs=in_specs, out_specs=[pl.BlockSpec((tm, tn), lambda i, j: (i, j))] * len(out_dtypes),
        out_shape=[jax.ShapeDtypeStruct((M, N), dt) for dt in out_dtypes],
        compiler_params=_params('parallel', 'parallel'),
    )(a, b, *extra)
    return outs[0] if len(out_dtypes) == 1 else tuple(outs)


def _rope_tables(pos, rot, period):
    half = rot // 2
    inv = jnp.power(jnp.float32(ROPE_THETA), -jnp.arange(half, dtype=F32) / half)
    ang = pos.astype(F32)[:, None] * inv[None, :]
    cos, sin = jnp.cos(ang), jnp.sin(ang)
    T = pos.shape[0]
    z = jnp.zeros((T, period - rot), F32)
    zh = jnp.zeros((T, half), F32)
    c = jnp.concatenate([cos, cos, jnp.ones((T, period - rot), F32)], axis=1)
    s1 = jnp.concatenate([-sin, zh, z], axis=1)
    s2 = jnp.concatenate([zh, sin, z], axis=1)
    rep = 128 // period
    return tuple(jnp.tile(t, (1, rep)) for t in (c, s1, s2))


def _split_w_in(w):
    parts, off = {}, 0
    for name, width in _IN_SPLITS:
        parts[name] = w[:, off:off + width]
        off += width
    cat = lambda names, total: jnp.pad(
        jnp.concatenate([parts[n] for n in names], axis=1),
        ((0, 0), (0, total - sum(parts[n].shape[1] for n in names))))
    zeros_ki = jnp.zeros_like(parts['k_i'])
    w64 = jnp.concatenate([parts['q_i'], parts['k_i'], zeros_ki, zeros_ki, parts['k_i']], axis=1)
    assert w64.shape[1] == W_R64
    return tuple(_with_bf16(g) for g in (cat(_GROUP_ROPE128, 3072), w64, cat(_GROUP_PLAIN, W_PLAIN), parts['g_mix']))


def _project(h, wsplit, tabs128, tabs64, precise=False):
    w128, w64, wpl, wg = (w[0] if precise else w[1] for w in wsplit)
    both = ('f32',) if precise else ('f32', 'bf16')
    z128 = _mm(h, w128, 512, 'rope', ROT_DIM // 2, tabs128, out_kinds=both, precise=precise, name='proj_rope128')
    z64 = _mm(h, w64, 640, 'rope', ROT_IDX // 2, tabs64, out_kinds=both + (() if precise else ('lo',)), precise=precise,
              name='proj_rope64')
    zpl = _mm(h, wpl, 640, out_kinds=both, precise=precise, name='proj_plain')
    zg = _mm(h, wg, 512, 'sigmoid', precise=precise, name='proj_gmix')
    return z128, z64, zpl, zg


def _stack_heads(q_ref, kv):
    return jnp.concatenate([q_ref[0, :, (kv * G + g) * DH:(kv * G + g + 1) * DH] for g in range(G)], axis=0).astype(BF16)


def _masked_attend(q, k, v, mask):
    s = _dg_nt(q, k) * SCALE
    p = jnp.concatenate([_mprobs(s[g * Q_BLK:(g + 1) * Q_BLK], mask).astype(BF16) for g in range(G)], axis=0)
    return jnp.dot(p, v, preferred_element_type=F32)


def _dsa_prompt_body(qa_ref, ka_ref, va_ref, qih_ref, qil_ref, kih_ref, kil_ref, ki2h_ref, ki2l_ref, wi_ref, o_ref,
                     *, k_sel, ext):
    i = pl.program_id(1)
    w = wi_ref[0][:, :H_I] * (H_I ** -0.5)
    npair = H_I * D_IDX // 128
    pairs = lambda r: jnp.concatenate([r[0, :, j * 128:(j + 1) * 128] for j in range(npair)], axis=0)
    qh = pairs(qih_ref)
    q3 = jnp.concatenate([qh, qh, pairs(qil_ref)], axis=1)
    parts = []
    for c0 in range(0, ext, KEY_CLS):
        rows = lambda r1, r2: jnp.concatenate([r1[0, c0:c0 + KEY_CLS, :], r2[0, c0:c0 + KEY_CLS, :]], axis=0)
        kh = rows(kih_ref, ki2h_ref)
        d = _dg_nt(q3, jnp.concatenate([kh, rows(kil_ref, ki2l_ref), kh], axis=1)) * (D_IDX ** -0.5)
        sc = None
        for j in range(npair):
            dj = d[j * Q_BLK:(j + 1) * Q_BLK]
            t = (jnp.maximum(dj[:, :KEY_CLS], 0.0) * w[:, 2 * j:2 * j + 1]
                 + jnp.maximum(dj[:, KEY_CLS:], 0.0) * w[:, 2 * j + 1:2 * j + 2])
            sc = t if sc is None else sc + t
        parts.append(sc)
    score = jnp.concatenate(parts, axis=1)
    qpos = i * Q_BLK + lax.broadcasted_iota(I32, (Q_BLK, 1), 0)
    kpos = lax.broadcasted_iota(I32, (1, ext), 1)
    masked = jnp.where(kpos <= qpos, score, NEG)
    key = jnp.where(masked > NEG * 0.5, _order_key(masked), INT_MIN)
    count_ge = lambda c: jnp.sum(jnp.where(key >= c, 1.0, 0.0), axis=1, keepdims=True)
    n_it = jnp.where((i + 1) * Q_BLK > k_sel, 32, 0)
    thr = _kth_largest_key(count_ge, Q_BLK, k_sel, n_it)
    sel = key >= jnp.maximum(thr, INT_MIN + 1)
    for kv in range(KVH_A):
        sl = slice(kv * DH, (kv + 1) * DH)
        o = _masked_attend(_stack_heads(qa_ref, kv), ka_ref[0, :ext, sl], va_ref[0, :ext, sl], sel)
        for g in range(G):
            o_ref[0, :, (kv * G + g) * DH:(kv * G + g + 1) * DH] = o[g * Q_BLK:(g + 1) * Q_BLK]


def _by_key_class(body, S, **kw):
    def kern(*refs):
        i = pl.program_id(1)
        for c in range(S // KEY_CLS):
            pl.when(i // (KEY_CLS // Q_BLK) == c)(functools.partial(body, *refs, ext=(c + 1) * KEY_CLS, **kw))
    return kern


def _dsa_prompt(z128h, z64h, z64l, zplh, zpl, B, S):
    k_sel = min(TOPK_MAX, S // 4)
    blk = lambda w, f: pl.BlockSpec((1, Q_BLK, w), f)
    full = lambda w, c: pl.BlockSpec((1, S, w), lambda b, i: (b, 0, c))
    return pl.pallas_call(
        _by_key_class(_dsa_prompt_body, S, k_sel=k_sel), grid=(B, S // Q_BLK), name='dsa_prompt',
        in_specs=[blk(1024, lambda b, i: (b, i, 0)), full(256, 8), full(256, 0),
                  blk(1024, lambda b, i: (b, i, 0)), blk(1024, lambda b, i: (b, i, 0)),
                  full(128, 8), full(128, 8), full(128, 9), full(128, 9), blk(128, lambda b, i: (b, i, 8))],
        out_specs=blk(1024, lambda b, i: (b, i, 0)),
        out_shape=jax.ShapeDtypeStruct((B, S, H_A * DH), F32),
        compiler_params=_params('parallel', 'parallel'),
    )(z128h, z128h, zplh, z64h, z64l, z64h, z64l, z64h, z64l, zpl)


def _cmp_partial_kernel(*refs, n_src, paged, precise):
    refs = refs[len(refs) - (2 * n_src + 8):]
    rw = KVH_B * DH

    def mm(x, w_ref, p):
        if not precise:
            return _dot(x, w_ref[3 * p * DH:(3 * p + 1) * DH, :])
        xh, xl = _split(x)
        return jnp.dot(jnp.concatenate([xh, xl, xh], axis=1), w_ref[3 * p * DH:3 * (p + 1) * DH, :], preferred_element_type=F32)

    def rows_of(r, p, kvh):
        if paged:
            return r[0, 0, pl.ds(KVH_B * p + kvh, PAGE_SIZE // S_CMP, stride=KVH_B * S_CMP), :]
        return r[0, :, p * rw + kvh * DH:p * rw + (kvh + 1) * DH]
    ksrc, vsrc = refs[:n_src], refs[n_src:2 * n_src]
    pek, w1k, pev, w1v, ak, bk, av, bv = refs[2 * n_src:]
    half = L_CMP // 2
    for srcs, pe_ref, w_ref, a_ref, b_ref in ((ksrc, pek, w1k, ak, bk), (vsrc, pev, w1v, av, bv)):
        for kvh in range(KVH_B):
            acc_a = acc_b = None
            for p in range(half):
                rows = jnp.concatenate([rows_of(r, p, kvh) for r in srcs], axis=0)
                da = mm(rows + pe_ref[p:p + 1, :], w_ref, p)
                db = mm(rows + pe_ref[half + p:half + p + 1, :], w_ref, half + p)
                acc_a = da if acc_a is None else acc_a + da
                acc_b = db if acc_b is None else acc_b + db
            a_ref[0, :, kvh * DH:(kvh + 1) * DH] = acc_a
            b_ref[0, :, kvh * DH:(kvh + 1) * DH] = acc_b


def _cmp_finish_kernel(ak_ref, bk_ref, av_ref, bv_ref, w2k_ref, w2v_ref, kc_ref, vc_ref, *, precise):
    n = ak_ref.shape[1]
    mm = _dot3 if precise else _dot
    for a_ref, b_ref, w2_ref, o_ref in ((ak_ref, bk_ref, w2k_ref, kc_ref), (av_ref, bv_ref, w2v_ref, vc_ref)):
        for kvh in range(KVH_B):
            sl = slice(kvh * DH, (kvh + 1) * DH)
            h = a_ref[0, :, sl] + pltpu.roll(b_ref[0, :, sl], n - 1, 0)
            o_ref[0, :, sl] = mm(jax.nn.gelu(h), w2_ref[...])


def _cmp_finish(parts, w2k, w2v, precise):
    Bt, n, _ = parts[0].shape
    spec = pl.BlockSpec((1, n, KVH_B * DH), lambda b: (b, 0, 0))
    wspec = pl.BlockSpec((DH, DH), lambda b: (0, 0))
    return pl.pallas_call(
        functools.partial(_cmp_finish_kernel, precise=precise), grid=(Bt,), name='cmp_finish',
        in_specs=[spec] * 4 + [wspec] * 2, out_specs=[spec] * 2,
        out_shape=[jax.ShapeDtypeStruct((Bt, n, KVH_B * DH), F32)] * 2,
        compiler_params=_params('parallel'),
    )(*parts, w2k, w2v)


def _cmp_weight_specs(imap):
    return [pl.BlockSpec((L_CMP, DH), imap), pl.BlockSpec((3 * L_CMP * DH, DH), imap)] * 2


def _cmp_w1_slabs(w1):
    hi = w1.astype(BF16)
    lo = (w1 - hi.astype(F32)).astype(BF16)
    r = lambda t: t.reshape(L_CMP, DH, DH)
    return jnp.concatenate([r(hi), r(hi), r(lo)], axis=1).reshape(3 * L_CMP * DH, DH)


def _compress_prompt(z128, zpl, cw, B, S):
    pek, w1k, w2k, pev, w1v, w2v = cw
    nch = S // S_CMP
    const = lambda b: (0, 0)
    out_spec = pl.BlockSpec((1, nch, KVH_B * DH), lambda b: (b, 0, 0))
    src_spec = pl.BlockSpec((1, nch, S_CMP * 256), lambda b: (b, 0, 0))
    chunked = lambda z, c: z[:, :, c:c + 256].reshape(B, nch, S_CMP * 256)
    parts = pl.pallas_call(
        functools.partial(_cmp_partial_kernel, n_src=1, paged=False, precise=False), grid=(B,), name='cmp_partial_prompt',
        in_specs=[src_spec, src_spec] + _cmp_weight_specs(const),
        out_specs=[out_spec] * 4,
        out_shape=[jax.ShapeDtypeStruct((B, nch, KVH_B * DH), F32)] * 4,
        compiler_params=_params('parallel'),
    )(chunked(z128, 2304), chunked(zpl, 256), pek, _cmp_w1_slabs(w1k), pev, _cmp_w1_slabs(w1v))
    return _cmp_finish(parts, w2k, w2v, precise=False)


def _nsa_prompt_body(qb_ref, kc_ref, vc_ref, ks_ref, vs_ref, kw_ref, vw_ref, gt_ref, ov_ref, ex_ref, o_ref,
                     *, n_cmp, n_slc, n_sel, ext):
    i = pl.program_id(1)
    S = ks_ref.shape[1]
    wspan = WINDOW + Q_BLK
    qpos = i * Q_BLK + lax.broadcasted_iota(I32, (Q_BLK, 1), 0)
    lane = lax.broadcasted_iota(I32, (1, 128), 1)
    kpos = lax.broadcasted_iota(I32, (1, ext), 1)
    causal = kpos <= qpos
    vis = (lane * S_CMP + (L_CMP - 1) <= qpos) & (lane < n_cmp)
    jt = qpos // L_SLC
    forced = (lane == 0) | (lane == jt) | (lane == jt - 1)
    wstart = pl.multiple_of(jnp.clip(i * Q_BLK - WINDOW, 0, S - wspan), Q_BLK)
    wpos = wstart + lax.broadcasted_iota(I32, (1, wspan), 1)
    wmask = (wpos <= qpos) & (wpos > qpos - WINDOW)
    gates = jax.nn.sigmoid(gt_ref[0])
    for kv in range(KVH_B):
        sl = slice(kv * DH, (kv + 1) * DH)
        q = _stack_heads(qb_ref, kv)
        s_cmp = _dg_nt(q, kc_ref[0, :, sl].astype(BF16)) * SCALE
        prs = [_mprobs(s_cmp[g * Q_BLK:(g + 1) * Q_BLK], vis) for g in range(G)]
        o_cmp = _dot(jnp.concatenate(prs, axis=0), vc_ref[0, :, sl])
        imp = jnp.dot(jnp.concatenate(_split(prs[0] + prs[1] + prs[2] + prs[3]), axis=1), ov_ref[...],
                      preferred_element_type=F32)
        sc = jnp.where(lane > jt, NEG, jnp.where(forced, FORCED, imp))
        rank = jnp.zeros((Q_BLK, 128), F32)
        for j2 in range(n_slc):
            col = sc[:, j2:j2 + 1]
            rank = rank + jnp.where(col > sc, 1.0, jnp.where((col == sc) & (lane > j2), 1.0, 0.0))
        sel = jnp.where((rank < n_sel) & (sc > NEG * 0.5), 1.0, 0.0).astype(BF16)
        tokmask = jnp.where(causal, jnp.dot(sel, ex_ref[:, :ext], preferred_element_type=F32), 0.0) > 0.5
        o_slc = _masked_attend(q, ks_ref[0, :ext, sl], vs_ref[0, :ext, sl], tokmask)
        o_win = _masked_attend(q, kw_ref[0, pl.ds(wstart, wspan), sl], vw_ref[0, pl.ds(wstart, wspan), sl], wmask)
        for g in range(G):
            hd = kv * G + g
            rows = slice(g * Q_BLK, (g + 1) * Q_BLK)
            c0 = GATE_OFF + hd * 3
            o_ref[0, :, hd * DH:(hd + 1) * DH] = (gates[:, c0:c0 + 1] * o_cmp[rows] + gates[:, c0 + 1:c0 + 2] * o_slc[rows]
                                                  + gates[:, c0 + 2:c0 + 3] * o_win[rows])


def _overlap_np(nc, nslc, rows, cols):
    cs = np.arange(rows)[:, None] * S_CMP
    js = np.arange(cols)[None, :] * L_SLC
    ov = (cs < js + L_SLC) & (cs + L_CMP > js) & (np.arange(rows)[:, None] < nc) & (np.arange(cols)[None, :] < nslc)
    return jnp.asarray(ov.astype(np.float32), dtype=BF16)


def _nsa_prompt(z128h, zplh, zpl, kc, vc, B, S):
    n_cmp = S // S_CMP - L_CMP // S_CMP + 1
    n_slc = -(-S // L_SLC)
    assert n_cmp <= 128 and n_slc <= 128 and kc.shape[1] == 128
    ov = _overlap_np(n_cmp, n_slc, 128, 128)
    ov = jnp.concatenate([ov, ov], axis=0)
    ex = jnp.asarray((np.arange(128)[:, None] == np.arange(S)[None, :] // L_SLC).astype(np.float32), dtype=BF16)
    blk = lambda w, f: pl.BlockSpec((1, Q_BLK, w), f)
    full = lambda w, c: pl.BlockSpec((1, S, w), lambda b, i: (b, 0, c))
    const = lambda shape: pl.BlockSpec(shape, lambda b, i: (0, 0))
    cspec = pl.BlockSpec((1, 128, 256), lambda b, i: (b, 0, 0))
    return pl.pallas_call(
        _by_key_class(_nsa_prompt_body, S, n_cmp=n_cmp, n_slc=n_slc, n_sel=min(N_SEL_MAX, n_slc)),
        grid=(B, S // Q_BLK), name='nsa_prompt',
        in_specs=[blk(1024, lambda b, i: (b, i, 1)), cspec, cspec, full(256, 10), full(256, 2), full(256, 11), full(256, 3),
                  blk(128, lambda b, i: (b, i, 8)), const((256, 128)), const((128, S))],
        out_specs=blk(1024, lambda b, i: (b, i, 0)),
        out_shape=jax.ShapeDtypeStruct((B, S, H_B * DH), F32),
        compiler_params=_params('parallel', 'parallel'),
    )(z128h, kc, vc, z128h, zplh, z128h, zplh, zpl, ov, ex)


def _merge_u_kernel(*refs, precise):
    oa_ref, ob_ref, wa_ref, wb_ref, ga_ref, gb_ref, o_ref = refs
    mm = _dot3 if precise else _dot
    a, b = mm(oa_ref[...], wa_ref[...]), mm(ob_ref[...], wb_ref[...])
    o_ref[...] = (ga_ref[...] * a + gb_ref[...] * b).astype(o_ref.dtype)


def _merge(x, o_a, o_b, zg, wa, wb, wo, precise=False):
    M = x.shape[0]
    tm, tn = min(M, 512), 512
    nj = D_MODEL // tn
    wspec = lambda k: pl.BlockSpec((k, tn), lambda i, j: (0, j))
    ws = (wa[0], wb[0]) if precise else (wa[1], wb[1])
    wspecs = [wspec(H_A * DH), wspec(H_B * DH)]
    u = pl.pallas_call(
        functools.partial(_merge_u_kernel, precise=precise), grid=(M // tm, nj), name='merge_u',
        in_specs=[pl.BlockSpec((tm, H_A * DH), lambda i, j: (i, 0)), pl.BlockSpec((tm, H_B * DH), lambda i, j: (i, 0))] + wspecs
        + [pl.BlockSpec((tm, tn), lambda i, j: (i, j)), pl.BlockSpec((tm, tn), lambda i, j: (i, j + nj))],
        out_specs=pl.BlockSpec((tm, tn), lambda i, j: (i, j)),
        out_shape=jax.ShapeDtypeStruct((M, D_MODEL), F32 if precise else BF16),
        compiler_params=_params('parallel', 'parallel'),
    )(o_a, o_b, *ws, zg, zg)
    return _mm(u, wo[0] if precise else wo[1], 512, 'res', extra=(x,), precise=precise, name='merge_out')


def _ffn_kernel(te_ref, nu_ref, tr_ref, h_ref, *rest, has_res, n_var, precise):
    nw = 3
    w = rest[:nw]
    o_ref, acc_ref = rest[-2], rest[-1]
    m, f = pl.program_id(0), pl.program_id(1)
    tm = acc_ref.shape[0]
    valid = tr_ref[m]

    @pl.when(f == 0)
    def _():
        acc_ref[...] = jnp.zeros_like(acc_ref)

    def part(rows):
        def run():
            h = h_ref[:rows, :]
            if precise:
                a = _dot3(h, w[0][0])
                b = _dot3(h, w[1][0])
                acc_ref[:rows, :] += _dot3(a * jax.nn.sigmoid(a) * b, w[2][0])
            else:
                a = jnp.dot(h, w[0][0], preferred_element_type=F32)
                b = jnp.dot(h, w[1][0], preferred_element_type=F32)
                acc_ref[:rows, :] += _dot(a * jax.nn.sigmoid(a) * b, w[2][0])
        return run

    step = tm // n_var
    for c in range(n_var):
        pl.when((valid > c * step) & (valid <= (c + 1) * step))(part((c + 1) * step))

    @pl.when(f == pl.num_programs(1) - 1)
    def _():
        o_ref[...] = acc_ref[...] + rest[nw][...] if has_res else acc_ref[...]


def _ffn(h, wg, wu, wd, tile_expert, n_used, tile_rows, res=None, precise=False):
    R, D = h.shape
    F = wg.shape[2]
    tm, tf = min(R, FFN_TM), 512
    nf = F // tf
    fe = lambda m, f, nu: jnp.where(m < nu[0], f, nf - 1)
    up_spec = pl.BlockSpec((1, D, tf), lambda m, f, te, nu, tr: (te[m], 0, fe(m, f, nu)))
    down_spec = pl.BlockSpec((1, tf, D), lambda m, f, te, nu, tr: (te[m], fe(m, f, nu), 0))
    in_specs = [pl.BlockSpec((tm, D), lambda m, f, te, nu, tr: (m, 0)), up_spec, up_spec, down_spec]
    args = [h, wg, wu, wd]
    if res is not None:
        in_specs.append(pl.BlockSpec((tm, D), lambda m, f, te, nu, tr: (m, 0)))
        args.append(res)
    return pl.pallas_call(
        functools.partial(_ffn_kernel, has_res=res is not None, n_var=4 if tm == FFN_TM else 1, precise=precise), name='ffn',
        grid_spec=pltpu.PrefetchScalarGridSpec(
            num_scalar_prefetch=3, grid=(R // tm, nf), in_specs=in_specs,
            out_specs=pl.BlockSpec((tm, D), lambda m, f, te, nu, tr: (m, 0)),
            scratch_shapes=[pltpu.VMEM((tm, D), F32)]),
        out_shape=jax.ShapeDtypeStruct((R, D), F32),
        compiler_params=_params('parallel', 'arbitrary'),
    )(tile_expert, n_used, tile_rows, *args)


def _dense_ffn(x, g, wg, wu, wd, precise=False):
    h = _rms(x, g, F32 if precise else BF16)
    tm = min(x.shape[0], FFN_TM)
    n_tiles = x.shape[0] // tm
    pick = lambda w: w[0 if precise else 1][None]
    return _ffn(h, pick(wg), pick(wu), pick(wd), jnp.zeros((n_tiles,), I32), jnp.full((1,), n_tiles, I32),
                jnp.full((n_tiles,), tm, I32), res=x, precise=precise)


def _router_kernel(x_ref, g_ref, w_ref, h_ref, idx_ref, gate_ref):
    x = x_ref[...]
    h = x * lax.rsqrt(jnp.mean(x * x, axis=-1, keepdims=True) + EPS) * g_ref[...]
    h_ref[...] = h.astype(BF16)
    logits = _dot3(h, w_ref[...])
    lane = lax.broadcasted_iota(I32, logits.shape, 1)
    lanef = lane.astype(F32)
    l1 = jnp.where(lane < N_EXP, logits, -jnp.inf)
    m1 = jnp.max(l1, axis=1, keepdims=True)
    i1 = jnp.min(jnp.where(l1 == m1, lanef, 128.0), axis=1, keepdims=True)
    l2 = jnp.where(lanef == i1, -jnp.inf, l1)
    m2 = jnp.max(l2, axis=1, keepdims=True)
    i2 = jnp.min(jnp.where(l2 == m2, lanef, 128.0), axis=1, keepdims=True)
    e = jnp.exp(m2 - m1)
    inv = 1.0 / (1.0 + e)
    idx_ref[...] = jnp.where(lane == 0, i1, i2).astype(I32)
    gate_ref[...] = jnp.where(lane == 0, inv, e * inv)


def _router(x, g, w_r):
    M, D = x.shape
    tm = min(M, 512)
    wpad = jnp.pad(w_r, ((0, 0), (0, 128 - N_EXP)))
    row = lambda w, dt: (pl.BlockSpec((tm, w), lambda i: (i, 0)), jax.ShapeDtypeStruct((M, w), dt))
    outs = [row(D, BF16), row(128, I32), row(128, F32)]
    h, idx, gate = pl.pallas_call(
        _router_kernel, grid=(M // tm,), name='router',
        in_specs=[pl.BlockSpec((tm, D), lambda i: (i, 0)), pl.BlockSpec((1, D), lambda i: (0, 0)),
                  pl.BlockSpec((D, 128), lambda i: (0, 0))],
        out_specs=[o[0] for o in outs], out_shape=[o[1] for o in outs],
        compiler_params=_params('parallel'),
    )(x, g.reshape(1, D), wpad)
    return h, idx[:, :2], gate[:, :2]


def _moe(xp, xs, g, w_r, wg, wu, wd):
    hp, ip, gp = _router(xp, g, w_r)
    hs, is_, gs = _router(xs, g, w_r)
    h = jnp.concatenate([hp, hs], axis=0)
    idx = jnp.concatenate([ip, is_], axis=0)
    gate = jnp.concatenate([gp, gs], axis=0)
    M = h.shape[0]
    tm = FFN_TM
    n_tiles = (2 * M + N_EXP * (tm - 1) + tm - 1) // tm
    e = jnp.concatenate([idx[:, 0], idx[:, 1]])
    onehot = (e[:, None] == jnp.arange(N_EXP, dtype=I32)[None, :]).astype(I32)
    rank = jnp.sum((jnp.cumsum(onehot, axis=0) - onehot) * onehot, axis=1)
    counts = jnp.sum(onehot, axis=0)
    tiles_per = (counts + tm - 1) // tm
    tile_end = jnp.cumsum(tiles_per)
    row_start = (tile_end - tiles_per) * tm
    pos = jnp.sum(onehot * row_start[None, :], axis=1) + rank
    src_tok = jnp.zeros((n_tiles * tm,), I32).at[pos].set(jnp.arange(2 * M, dtype=I32) % M)
    n_used = tile_end[-1].astype(I32)
    tid = jnp.arange(n_tiles, dtype=I32)
    t = jnp.minimum(tid, n_used - 1)
    tile_expert = jnp.sum((t[:, None] >= tile_end[None, :]).astype(I32), axis=1)
    rows_left = counts[tile_expert] - (t - (tile_end - tiles_per)[tile_expert]) * tm
    tile_rows = jnp.where(tid < n_used, jnp.clip(rows_left, 0, tm), 0).astype(I32)
    y = _ffn(jnp.take(h, src_tok, axis=0), wg, wu, wd, tile_expert, n_used.reshape(1), tile_rows)
    f = gate[:, 0:1] * jnp.take(y, pos[:M], axis=0) + gate[:, 1:2] * jnp.take(y, pos[M:], axis=0)
    Mp = xp.shape[0]
    return xp + f[:Mp], xs + f[Mp:]


def _page_specs(block, n_lead, layer, col=None):
    def spec(j):
        def imap(b, s, pt):
            page = pt[b * (PAST_LEN // PAGE_SIZE) + s * PAGES_PER_STEP + j]
            return (layer, page) + (0,) * n_lead
        return pl.BlockSpec(block, imap)
    return [spec(j) for j in range(PAGES_PER_STEP)]


def _idx_walk_kernel(pt_ref, q_ref, w_ref, *refs):
    pages, o_ref = refs[:-1], refs[-1]
    keys = jnp.concatenate([r[0, 0] for r in pages], axis=0)
    d = _dot3_nt(q_ref[0], keys) * (D_IDX ** -0.5)
    o_ref[0, 0] = jnp.sum(jnp.maximum(d, 0.0) * (w_ref[0] * (H_I ** -0.5)), axis=0, keepdims=True)


def _dsa_thresh_kernel(sc_ref, qi_ref, wi_ref, kin_ref, sel_ref, selnew_ref, *, k_sel):
    nb = sc_ref.shape[0]
    snew = []
    for b in range(nb):
        d = jnp.sum(qi_ref[b] * kin_ref[b:b + 1, :], axis=1, keepdims=True) * (D_IDX ** -0.5)
        snew.append(jnp.sum(jnp.maximum(d, 0.0) * (wi_ref[b] * (H_I ** -0.5)), axis=0, keepdims=True))
    snew = jnp.concatenate(snew, axis=0)
    key, knew = _order_key(sc_ref[...]), _order_key(snew)
    count_ge = lambda c: (jnp.sum(jnp.where(key >= c, 1.0, 0.0), axis=1, keepdims=True) + jnp.where(knew >= c, 1.0, 0.0))
    thr = _kth_largest_key(count_ge, nb, k_sel)
    sel_ref[...] = jnp.where((key >= thr) & (sc_ref[...] > NEG * 0.5), 1.0, 0.0)
    selnew_ref[...] = jnp.broadcast_to(jnp.where((knew >= thr) & (snew > NEG * 0.5), 1.0, 0.0), selnew_ref.shape)


def _dsa_walk_kernel(pt_ref, q_ref, sel_ref, selnew_ref, knew_ref, vnew_ref, *refs):
    n = PAGES_PER_STEP
    kpages, vpages = refs[:n], refs[n:2 * n]
    o_ref, m_ref, l_ref, acc_ref = refs[2 * n:]
    s_id = pl.program_id(1)
    row = lax.broadcasted_iota(I32, (H_A, 1), 0)
    first = row < G

    @pl.when(s_id == 0)
    def _():
        m_ref[...] = jnp.full_like(m_ref, NEG)
        l_ref[...] = jnp.zeros_like(l_ref)
        acc_ref[...] = jnp.zeros_like(acc_ref)

    q = q_ref[0]
    k = jnp.concatenate([r[0, 0] for r in kpages], axis=0).astype(BF16)
    v = jnp.concatenate([r[0, 0] for r in vpages], axis=0).astype(BF16)
    col = lax.broadcasted_iota(I32, (1, k.shape[0]), 1)
    mask = jnp.where((col & 1) == jnp.where(first, 0, 1), sel_ref[0, 0], 0.0) > 0.5
    s = jnp.where(mask, _dot_nt(q, k) * SCALE, NEG)
    m_new = jnp.maximum(m_ref[...], jnp.max(s, axis=1, keepdims=True))
    p = jnp.where(mask, jnp.exp(s - m_new), 0.0)
    alpha = jnp.exp(m_ref[...] - m_new)
    l_ref[...] = alpha * l_ref[...] + jnp.sum(p, axis=1, keepdims=True)
    acc_ref[...] = alpha * acc_ref[...] + jnp.dot(p.astype(BF16), v, preferred_element_type=F32)
    m_ref[...] = m_new

    @pl.when(s_id == pl.num_programs(1) - 1)
    def _():
        kn = jnp.where(first, knew_ref[0][:, :DH], knew_ref[0][:, DH:])
        vn = jnp.where(first, vnew_ref[0][:, :DH], vnew_ref[0][:, DH:])
        valid = selnew_ref[0][:, :1] > 0.5
        sn = jnp.where(valid, jnp.sum(q * kn, axis=1, keepdims=True) * SCALE, NEG)
        m_fin = jnp.maximum(m_ref[...], sn)
        pn = jnp.where(valid, jnp.exp(sn - m_fin), 0.0)
        a2 = jnp.exp(m_ref[...] - m_fin)
        l_fin = a2 * l_ref[...] + pn
        o_ref[0] = (a2 * acc_ref[...] + pn * vn) * (1.0 / jnp.maximum(l_fin, 1e-30))


def _dsa_sample(l, z128, z64, zpl, cache_k, cache_v, cache_ki, pt_flat, DB):
    n_pages = PAST_LEN // PAGE_SIZE
    n_steps = n_pages // PAGES_PER_STEP
    span = PAGES_PER_STEP * PAGE_SIZE
    k_sel = min(TOPK_MAX, (PAST_LEN + 1) // 4)
    qi = z64[:DB, :H_I * D_IDX].reshape(DB, H_I, D_IDX)
    wi = zpl[:DB, 1024:1024 + H_I].reshape(DB, H_I, 1)
    kin = z64[:DB, 1024:1024 + D_IDX]
    scores = pl.pallas_call(
        _idx_walk_kernel, name='idx_walk',
        grid_spec=pltpu.PrefetchScalarGridSpec(
            num_scalar_prefetch=1, grid=(DB, n_steps),
            in_specs=[pl.BlockSpec((1, H_I, D_IDX), lambda b, s, pt: (b, 0, 0)), pl.BlockSpec((1, H_I, 1), lambda b, s, pt: (b, 0, 0))]
            + _page_specs((1, 1, PAGE_SIZE, D_IDX), 2, l),
            out_specs=pl.BlockSpec((1, 1, 1, span), lambda b, s, pt: (b, s, 0, 0))),
        out_shape=jax.ShapeDtypeStruct((DB, n_steps, 1, span), F32),
        compiler_params=_params('parallel', 'parallel'),
    )(pt_flat, qi, wi, *([cache_ki] * PAGES_PER_STEP))
    sel, selnew = pl.pallas_call(
        functools.partial(_dsa_thresh_kernel, k_sel=k_sel), name='dsa_thresh',
        out_shape=[jax.ShapeDtypeStruct((DB, PAST_LEN), F32), jax.ShapeDtypeStruct((DB, 128), F32)],
        compiler_params=pltpu.CompilerParams(vmem_limit_bytes=VMEM_LIMIT),
    )(scores.reshape(DB, PAST_LEN), qi, wi, kin)
    qa = z128[:DB, :H_A * DH].reshape(DB, H_A, DH)
    knew = jnp.broadcast_to(z128[:DB, None, 2048:2304], (DB, H_A, KVH_A * DH))
    vnew = jnp.broadcast_to(zpl[:DB, None, 0:256], (DB, H_A, KVH_A * DH))
    selnew = jnp.broadcast_to(selnew[:, None, :], (DB, H_A, 128))
    per_b = lambda shape: pl.BlockSpec(shape, lambda b, s, pt: (b, 0, 0))
    o = pl.pallas_call(
        _dsa_walk_kernel, name='dsa_walk',
        grid_spec=pltpu.PrefetchScalarGridSpec(
            num_scalar_prefetch=1, grid=(DB, n_steps),
            in_specs=[per_b((1, H_A, DH)), pl.BlockSpec((1, 1, 1, KVH_A * span), lambda b, s, pt: (b, s, 0, 0)),
                      per_b((1, H_A, 128)), per_b((1, H_A, KVH_A * DH)), per_b((1, H_A, KVH_A * DH))]
            + _page_specs((1, 1, KVH_A * PAGE_SIZE, DH), 2, l) * 2,
            out_specs=per_b((1, H_A, DH)),
            scratch_shapes=[pltpu.VMEM((H_A, 1), F32), pltpu.VMEM((H_A, 1), F32), pltpu.VMEM((H_A, DH), F32)]),
        out_shape=jax.ShapeDtypeStruct((DB, H_A, DH), F32),
        compiler_params=_params('parallel', 'arbitrary'),
    )(pt_flat, qa, jnp.repeat(sel, KVH_A, axis=1).reshape(DB, n_steps, 1, KVH_A * span), selnew, knew, vnew,
      *([cache_k] * PAGES_PER_STEP), *([cache_v] * PAGES_PER_STEP))
    return o.reshape(DB, H_A * DH)


def _compress_sample(l, cache_k, cache_v, cw, pt_flat, DB):
    pek, w1k, w2k, pev, w1v, w2v = cw
    n_pages = PAST_LEN // PAGE_SIZE
    n_steps = n_pages // PAGES_PER_STEP
    cps = PAGE_SIZE // S_CMP
    nch_step = PAGES_PER_STEP * cps
    const = lambda b, s, pt: (0, 0)
    out_spec = pl.BlockSpec((1, nch_step, KVH_B * DH), lambda b, s, pt: (b, s, 0))
    page_specs = _page_specs((1, 1, KVH_B * PAGE_SIZE, DH), 2, l)
    parts = pl.pallas_call(
        functools.partial(_cmp_partial_kernel, n_src=PAGES_PER_STEP, paged=True, precise=False), name='cmp_partial_sample',
        grid_spec=pltpu.PrefetchScalarGridSpec(
            num_scalar_prefetch=1, grid=(DB, n_steps),
            in_specs=page_specs + page_specs + _cmp_weight_specs(const),
            out_specs=[out_spec] * 4),
        out_shape=[jax.ShapeDtypeStruct((DB, PAST_LEN // S_CMP, KVH_B * DH), F32)] * 4,
        compiler_params=_params('parallel', 'parallel'),
    )(pt_flat, *([cache_k] * PAGES_PER_STEP), *([cache_v] * PAGES_PER_STEP), pek, _cmp_w1_slabs(w1k), pev, _cmp_w1_slabs(w1v))
    return _cmp_finish(parts, w2k, w2v, precise=True)


def _cmp_select_sample_kernel(q_ref, kc_ref, vc_ref, ov_ref, o_ref, ids_ref, *, n_cmp, n_slc, n_sel):
    n = kc_ref.shape[1]
    q = q_ref[0]
    row = lax.broadcasted_iota(I32, (H_B, 1), 0)
    first = row < G
    kc = kc_ref[0]
    vc = vc_ref[0]
    s = jnp.where(first, _dot3_nt(q, kc[:, :DH]), _dot3_nt(q, kc[:, DH:])) * SCALE
    col = lax.broadcasted_iota(I32, (1, n), 1)
    qpos = PAST_LEN
    pr = _mprobs(s, (col * S_CMP + (L_CMP - 1) <= qpos) & (col < n_cmp))
    o_ref[0] = jnp.where(first, _dot3(pr, vc[:, :DH]), _dot3(pr, vc[:, DH:]))
    psum = jnp.concatenate([jnp.sum(jnp.where((row >= kv * G) & (row < (kv + 1) * G), pr, 0.0), axis=0, keepdims=True)
                            for kv in range(KVH_B)] + [jnp.zeros((8 - KVH_B, n), F32)], axis=0)
    ph, plo = _split(psum)
    imp = jnp.dot(ph, ov_ref[...], preferred_element_type=F32) + jnp.dot(plo, ov_ref[...], preferred_element_type=F32)
    j = lax.broadcasted_iota(I32, imp.shape, 1)
    jf = j.astype(F32)
    jt = qpos // L_SLC
    forced = (j == 0) | (j == jt) | (j == jt - 1)
    sc = jnp.where((j > jt) | (j >= n_slc), NEG, jnp.where(forced, FORCED, imp))
    lane = lax.broadcasted_iota(I32, (8, 128), 1)
    ids = jnp.zeros((8, 128), F32)
    for r in range(n_sel):
        m = jnp.max(sc, axis=1, keepdims=True)
        pick = jnp.min(jnp.where(sc == m, jf, 1e9), axis=1, keepdims=True)
        ids = jnp.where(lane == r, pick, ids)
        sc = jnp.where(jf == pick, -jnp.inf, sc)
    ids_ref[0] = ids.astype(I32)


def _slc_sample_kernel(ids_ref, pt_ref, q_ref, knew_ref, vnew_ref, *refs, n_sel, n_past):
    kblks, vblks, o_ref = refs[:n_sel], refs[n_sel:2 * n_sel], refs[2 * n_sel]
    b, kv = pl.program_id(0), pl.program_id(1)
    base = (b * KVH_B + kv) * n_sel
    nrow = KVH_B * L_SLC
    row = lax.broadcasted_iota(I32, (nrow, 1), 0)
    lane = lax.broadcasted_iota(I32, (1, n_sel * nrow), 1)
    tpos = (lane % nrow) // KVH_B
    ks, vs = [], []
    for r in range(n_sel):
        bid = ids_ref[base + r]
        ks.append(jnp.where(bid < n_past, kblks[r][0, 0, 0], jnp.where(row == kv, knew_ref[0, 0], 0.0)))
        vs.append(jnp.where(bid < n_past, vblks[r][0, 0, 0], jnp.where(row == kv, vnew_ref[0, 0], 0.0)))
        tpos = tpos + jnp.where(lane // nrow == r, bid * L_SLC, 0)
    mask = jnp.where(lane % KVH_B == kv, tpos, PAST_LEN + 1) <= PAST_LEN
    s = _dot3_nt(q_ref[0, 0], jnp.concatenate(ks, axis=0)) * SCALE
    o_ref[0, 0] = _dot3(_mprobs(s, mask), jnp.concatenate(vs, axis=0))


def _win_sample_kernel(q_ref, kb_ref, vb_ref, kn_ref, vn_ref, o_ref, ko_ref, vo_ref):
    W = kb_ref.shape[2]
    q = q_ref[0]
    rowq = lax.broadcasted_iota(I32, (H_B, 1), 0)
    first = rowq < G
    kb, vb = kb_ref[0, 0], vb_ref[0, 0]
    kn, vn = kn_ref[0], vn_ref[0]
    s = jnp.where(first, _dot3_nt(q, kb[:, :DH]), _dot3_nt(q, kb[:, DH:])) * SCALE
    col = lax.broadcasted_iota(I32, (1, W), 1)
    mask = (PAST_LEN - W + col) > (PAST_LEN - WINDOW)
    knh = jnp.where(first, kn[:, :DH], kn[:, DH:])
    vnh = jnp.where(first, vn[:, :DH], vn[:, DH:])
    sn = jnp.sum(q * knh, axis=1, keepdims=True) * SCALE
    s = jnp.where(mask, s, NEG)
    m = jnp.maximum(jnp.max(s, axis=1, keepdims=True), sn)
    p = jnp.where(mask, jnp.exp(s - m), 0.0)
    pn = jnp.exp(sn - m)
    inv = 1.0 / jnp.maximum(jnp.sum(p, axis=1, keepdims=True) + pn, 1e-30)
    pn_ = p * inv
    pv = jnp.where(first, _dot3(pn_, vb[:, :DH]), _dot3(pn_, vb[:, DH:]))
    o_ref[0] = pv + (pn * inv) * vnh
    roww = lax.broadcasted_iota(I32, (W, 1), 0)
    ko_ref[0] = jnp.where(roww == W - 1, kn[:1, :], pltpu.roll(kb, W - 1, 0))
    vo_ref[0] = jnp.where(roww == W - 1, vn[:1, :], pltpu.roll(vb, W - 1, 0))


def _nsa_sample(l, z128, zpl, cache_ck, cache_cv, cache_sk, cache_sv, buf_k, buf_v, cw, pt_flat, DB):
    kc, vc = _compress_sample(l, cache_ck, cache_cv, cw, pt_flat, DB)
    n = kc.shape[1]
    L = PAST_LEN + 1
    n_cmp = L // S_CMP - L_CMP // S_CMP + 1
    n_slc = -(-L // L_SLC)
    n_sel = min(N_SEL_MAX, n_slc)
    n_past = PAST_LEN // L_SLC
    cols = -(-n_slc // 128) * 128
    qb = z128[:DB, 1024:2048].reshape(DB, H_B, DH)
    per_b = lambda shape: pl.BlockSpec(shape, lambda b: (b,) + (0,) * (len(shape) - 1))
    o_cmp, ids = pl.pallas_call(
        functools.partial(_cmp_select_sample_kernel, n_cmp=n_cmp, n_slc=n_slc, n_sel=n_sel), grid=(DB,), name='cmp_select_sample',
        in_specs=[per_b((1, H_B, DH)), per_b((1, n, 256)), per_b((1, n, 256)), pl.BlockSpec((n, cols), lambda b: (0, 0))],
        out_specs=[per_b((1, H_B, DH)), per_b((1, 8, 128))],
        out_shape=[jax.ShapeDtypeStruct((DB, H_B, DH), F32), jax.ShapeDtypeStruct((DB, 8, 128), I32)],
        compiler_params=_params('parallel'),
    )(qb, kc, vc, _overlap_np(n_cmp, n_slc, n, cols))
    ids_flat = ids[:, :KVH_B, :n_sel].reshape(-1)
    q4 = jnp.pad(qb.reshape(DB, KVH_B, G, DH), ((0, 0), (0, 0), (0, 8 - G), (0, 0)))
    knew = z128[:DB, 2560:2816].reshape(DB, KVH_B, 1, DH)
    vnew = zpl[:DB, 512:768].reshape(DB, KVH_B, 1, DH)
    bpp = PAGE_SIZE // L_SLC
    n_pages = PAST_LEN // PAGE_SIZE

    nrow = KVH_B * L_SLC

    def blk_spec(r):
        def imap(b, kv, ids_s, pt):
            bid = jnp.minimum(ids_s[(b * KVH_B + kv) * n_sel + r], n_past - 1)
            return (l, pt[b * n_pages + bid // bpp], bid % bpp, 0, 0)
        return pl.BlockSpec((1, 1, 1, nrow, DH), imap)

    per_bk = lambda shape: pl.BlockSpec(shape, lambda b, kv, ids_s, pt: (b, kv, 0, 0))
    view = lambda c: c.reshape(c.shape[0], c.shape[1], bpp, nrow, DH)
    blk_specs = [blk_spec(r) for r in range(n_sel)]
    o_slc = pl.pallas_call(
        functools.partial(_slc_sample_kernel, n_sel=n_sel, n_past=n_past), name='slc_sample',
        grid_spec=pltpu.PrefetchScalarGridSpec(
            num_scalar_prefetch=2, grid=(DB, KVH_B),
            in_specs=[per_bk((1, 1, 8, DH)), per_bk((1, 1, 1, DH)), per_bk((1, 1, 1, DH))] + blk_specs + blk_specs,
            out_specs=per_bk((1, 1, 8, DH))),
        out_shape=jax.ShapeDtypeStruct((DB, KVH_B, 8, DH), F32),
        compiler_params=_params('parallel', 'parallel'),
    )(ids_flat, pt_flat, q4, knew, vnew, *([view(cache_sk)] * n_sel), *([view(cache_sv)] * n_sel))
    W = buf_k.shape[2]
    kn8 = jnp.broadcast_to(z128[:DB, None, 2816:3072], (DB, 8, 256))
    vn8 = jnp.broadcast_to(zpl[:DB, None, 768:1024], (DB, 8, 256))
    bufspec = pl.BlockSpec((1, 1, W, 256), lambda b: (l, b, 0, 0))
    o_win, nk, nv = pl.pallas_call(
        _win_sample_kernel, grid=(DB,), name='win_sample',
        in_specs=[per_b((1, H_B, DH)), bufspec, bufspec, per_b((1, 8, 256)), per_b((1, 8, 256))],
        out_specs=[per_b((1, H_B, DH)), per_b((1, W, 256)), per_b((1, W, 256))],
        out_shape=[jax.ShapeDtypeStruct((DB, H_B, DH), F32)] + [jax.ShapeDtypeStruct((DB, W, 256), F32)] * 2,
        compiler_params=_params('parallel'),
    )(qb, buf_k.reshape(buf_k.shape[0], DB, W, 256), buf_v.reshape(buf_v.shape[0], DB, W, 256), kn8, vn8)
    gb = jax.nn.sigmoid(zpl[:DB, 1024 + GATE_OFF:1024 + GATE_OFF + 3 * H_B]).reshape(DB, H_B, 3)
    o_b = gb[..., 0:1] * o_cmp + gb[..., 1:2] * o_slc[:, :, :G].reshape(DB, H_B, DH) + gb[..., 2:3] * o_win
    return o_b.reshape(DB, H_B * DH), nk, nv


def kernel(x_prompt, x_sample, cache_dsa_k, cache_dsa_v, cache_idx_k, cache_cmp_k, cache_cmp_v, cache_slc_k, cache_slc_v, state_win_k, state_win_v, page_table, norm_mix, w_in, cmp_pe_k, cmp_w1_k, cmp_w2_k, cmp_pe_v, cmp_w1_v, cmp_w2_v, w_up_a, w_up_b, w_o, norm_ffn, dense_w_gate, dense_w_up, dense_w_down, moe_w_router, moe_w_gate, moe_w_up, moe_w_down, norm_final):
    B, S, D = x_prompt.shape
    DB, T, _ = x_sample.shape
    depth = w_in.shape[0]
    assert T == 1 and D == D_MODEL and page_table.shape == (DB, PAST_LEN // PAGE_SIZE)
    wp = min(WINDOW, S)
    xp = x_prompt.reshape(B * S, D)
    xs = jnp.pad(x_sample.reshape(DB, D), ((0, SAMPLE_ROWS - DB), (0, 0)))
    pt_flat = page_table.reshape(-1).astype(I32)
    pos_p = jnp.tile(jnp.arange(S), B)
    pos_s = jnp.full((SAMPLE_ROWS,), PAST_LEN)
    tabs_p = (_rope_tables(pos_p, ROT_DIM, DH), _rope_tables(pos_p, ROT_IDX, D_IDX))
    tabs_s = (_rope_tables(pos_s, ROT_DIM, DH), _rope_tables(pos_s, ROT_IDX, D_IDX))
    rows_view = lambda c: c.reshape(c.shape[0], c.shape[1], -1, c.shape[-1])
    c_dsa_k, c_dsa_v, c_idx, c_cmp_k, c_cmp_v, c_slc_k, c_slc_v = map(
        rows_view, (cache_dsa_k, cache_dsa_v, cache_idx_k, cache_cmp_k, cache_cmp_v, cache_slc_k, cache_slc_v))
    newp, news = [], []
    for l in range(depth):
        cw = (cmp_pe_k[l], cmp_w1_k[l], cmp_w2_k[l], cmp_pe_v[l], cmp_w1_v[l], cmp_w2_v[l])
        wsplit = _split_w_in(w_in[l])
        wa, wb, wo = _with_bf16(w_up_a[l]), _with_bf16(w_up_b[l]), _with_bf16(w_o[l])

        z128, z64, zpl, zg = _project(_rms(xp, norm_mix[l], BF16), wsplit, *tabs_p)
        (z128b, z128h), (z64b, z64h, z64l), (zplb, zplh) = (tuple(t.reshape(B, S, -1) for t in z) for z in (z128, z64, zpl))
        o_a = _dsa_prompt(z128h, z64h, z64l, zplh, zplb, B, S)
        kc, vc = _compress_prompt(z128b, zplb, cw, B, S)
        o_b = _nsa_prompt(z128h, zplh, zplb, kc, vc, B, S)
        xp = _merge(xp, o_a.reshape(B * S, -1), o_b.reshape(B * S, -1), zg, wa, wb, wo)
        kv4 = lambda z, c: z[:, :, c:c + 256].reshape(B, S, 2, DH)
        newp.append((kv4(z128b, 2048), kv4(zplb, 0), z64b[:, :, 1024:1024 + D_IDX], kv4(z128b, 2304), kv4(zplb, 256),
                     kv4(z128b, 2560), kv4(zplb, 512), kv4(z128b, 2816)[:, S - wp:], kv4(zplb, 768)[:, S - wp:]))

        y128, y64, ypl, yg = _project(_rms(xs, norm_mix[l], F32), wsplit, *tabs_s, precise=True)
        o_a = _dsa_sample(l, y128, y64, ypl, c_dsa_k, c_dsa_v, c_idx, pt_flat, DB)
        o_b, win_k, win_v = _nsa_sample(l, y128, ypl, c_cmp_k, c_cmp_v, c_slc_k, c_slc_v, state_win_k, state_win_v, cw, pt_flat, DB)
        pad = lambda o: jnp.pad(o, ((0, SAMPLE_ROWS - DB), (0, 0)))
        xs = _merge(xs, pad(o_a), pad(o_b), yg, wa, wb, wo, precise=True)
        s4 = lambda z, c: z[:DB, c:c + 256].reshape(DB, 1, 2, DH)
        news.append((s4(y128, 2048), s4(ypl, 0), y64[:DB, 1024:1024 + D_IDX].reshape(DB, 1, D_IDX), s4(y128, 2304), s4(ypl, 256),
                     s4(y128, 2560), s4(ypl, 512), win_k.reshape(DB, -1, 2, DH), win_v.reshape(DB, -1, 2, DH)))

        i = l // 2
        if l % 2 == 0:
            ws = (_with_bf16(dense_w_gate[i]), _with_bf16(dense_w_up[i]), _with_bf16(dense_w_down[i]))
            xp = _dense_ffn(xp, norm_ffn[l], *ws)
            xs = _dense_ffn(xs, norm_ffn[l], *ws, precise=True)
        else:
            xp, xs = _moe(xp, xs, norm_ffn[l], moe_w_router[i], moe_w_gate[i].astype(BF16), moe_w_up[i].astype(BF16),
                          moe_w_down[i].astype(BF16))

    y_prompt = _rms(xp, norm_final, F32).reshape(B, S, D)
    y_sample = _rms(xs, norm_final, F32)[:DB].reshape(DB, 1, D)
    stack = lambda items: [jnp.stack([it[n] for it in items], axis=0) for n in range(9)]
    return (y_prompt, y_sample, *stack(newp), *stack(news))
```
